```python
import math
import jax, jax.numpy as jnp
from jax import lax
import numpy as np

D_MODEL = 1024
BATCH = 8
SEQ = 2048
DEPTH = 1

GDN_HEAD_DIM = 128
GDN_HEADS = D_MODEL // GDN_HEAD_DIM
GDN_WIDTH = GDN_HEADS * GDN_HEAD_DIM
CONV_K = 4
CHUNK = 64
DIFF_HEAD_DIM = 64
DIFF_V_DIM = 2 * DIFF_HEAD_DIM
DIFF_HEADS = D_MODEL // DIFF_V_DIM
DIFF_QK_WIDTH = DIFF_HEADS * 2 * DIFF_HEAD_DIM
DIFF_WIDTH = DIFF_HEADS * DIFF_V_DIM
Q_BLOCK = 128
D_FF = -(-8 * D_MODEL // (3 * 256)) * 256
EPS = 1e-6

SPLIT_SIZES = (GDN_WIDTH, GDN_WIDTH, GDN_WIDTH, GDN_WIDTH, GDN_HEADS, GDN_HEADS,
               DIFF_QK_WIDTH, DIFF_QK_WIDTH, DIFF_WIDTH, D_MODEL, D_MODEL)
D_IN = sum(SPLIT_SIZES)
SPLIT_POINTS = tuple(sum(SPLIT_SIZES[:i + 1]) for i in range(len(SPLIT_SIZES) - 1))

kernel_name = "hybrid_gdn_diffattn_gated_merge_swiglu"


def rms_norm(x, w):
    xf = x.astype(jnp.float32)
    y = xf * lax.rsqrt(jnp.mean(xf * xf, axis=-1, keepdims=True) + EPS)
    return (y * w.astype(jnp.float32)).astype(x.dtype)


def l2_norm(x):
    return x * lax.rsqrt(jnp.sum(x * x, axis=-1, keepdims=True) + EPS)


def causal_depthwise_conv(x, w):
    return lax.conv_general_dilated(
        x, w[:, None, :], window_strides=(1,), padding=[(w.shape[0] - 1, 0)],
        dimension_numbers=('NWC', 'WIO', 'NWC'), feature_group_count=x.shape[-1])


def chunk_gated_delta_rule(q, k, v, g, beta):
    B, S, H, DK = q.shape
    DV = v.shape[-1]
    N = S // CHUNK

    def to_chunks(t):
        t = jnp.moveaxis(t, 2, 1)
        return t.reshape(t.shape[:2] + (N, CHUNK) + t.shape[3:])

    q, k, v, g, beta = (to_chunks(t) for t in (q * DK ** -0.5, k, v, g, beta))
    gc = jnp.cumsum(g, axis=-1)
    idx = jnp.arange(CHUNK)
    causal = idx[:, None] >= idx[None, :]
    strict = idx[:, None] > idx[None, :]
    decay = jnp.exp(jnp.where(causal, gc[..., :, None] - gc[..., None, :], -jnp.inf))
    kb = k * beta[..., None]
    lower = jnp.where(strict, jnp.einsum('bhnid,bhnjd->bhnij', kb, k) * decay, 0.0)
    eye = jnp.eye(CHUNK, dtype=q.dtype)
    t_inv = lax.linalg.triangular_solve(eye + lower, jnp.broadcast_to(eye, lower.shape),
                                        left_side=True, lower=True)
    u = t_inv @ (v * beta[..., None])
    w = t_inv @ (kb * jnp.exp(gc)[..., None])
    a_qk = jnp.einsum('bhnid,bhnjd->bhnij', q, k) * decay
    q_dec = q * jnp.exp(gc)[..., None]
    g_last = gc[..., -1]
    k_dec = k * jnp.exp(g_last[..., None] - gc)[..., None]

    def step(state, xs):
        u_n, w_n, qd_n, aqk_n, kd_n, gl_n = xs
        v_new = u_n - w_n @ state
        o_n = qd_n @ state + aqk_n @ v_new
        state = state * jnp.exp(gl_n)[..., None, None] + jnp.swapaxes(kd_n, -1, -2) @ v_new
        return state, o_n

    xs = tuple(jnp.moveaxis(t, 2, 0) for t in (u, w, q_dec, a_qk, k_dec, g_last))
    s0 = jnp.zeros((B, H, DK, DV), q.dtype)
    _, o = lax.scan(step, s0, xs)
    o = jnp.moveaxis(o, 0, 2).reshape(B, H, S, DV)
    return jnp.moveaxis(o, 1, 2)


def gated_delta_net(q, k, v, z, a, b, conv_w, a_log, dt_bias, norm_w):
    B, S, _ = q.shape
    f32 = jnp.float32
    qkv = jnp.concatenate([q, k, v], axis=-1).astype(f32)
    qkv = jax.nn.silu(causal_depthwise_conv(qkv, conv_w.astype(f32)))
    q, k, v = jnp.split(qkv, 3, axis=-1)
    q = l2_norm(q.reshape(B, S, GDN_HEADS, GDN_HEAD_DIM))
    k = l2_norm(k.reshape(B, S, GDN_HEADS, GDN_HEAD_DIM))
    v = v.reshape(B, S, GDN_HEADS, GDN_HEAD_DIM)
    beta = jax.nn.sigmoid(b.astype(f32))
    g = -jnp.exp(a_log.astype(f32)) * jax.nn.softplus(a.astype(f32) + dt_bias.astype(f32))
    o = chunk_gated_delta_rule(q, k, v, g, beta)
    o = rms_norm(o, norm_w) * jax.nn.silu(z.reshape(B, S, GDN_HEADS, GDN_HEAD_DIM).astype(f32))
    return o.reshape(B, S, GDN_WIDTH)


def diff_attention(q, k, v, q_norm_w, k_norm_w, lq1, lk1, lq2, lk2, subln_w, lambda_init):
    B, S, _ = q.shape
    f32 = jnp.float32
    q = rms_norm(q.reshape(B, S, DIFF_HEADS, 2, DIFF_HEAD_DIM), q_norm_w).astype(f32) * DIFF_HEAD_DIM ** -0.5
    k = rms_norm(k.reshape(B, S, DIFF_HEADS, 2, DIFF_HEAD_DIM), k_norm_w).astype(f32)
    v = v.reshape(B, S, DIFF_HEADS, DIFF_V_DIM).astype(f32)
    lam = (jnp.exp(jnp.sum(lq1.astype(f32) * lk1.astype(f32)))
           - jnp.exp(jnp.sum(lq2.astype(f32) * lk2.astype(f32))) + lambda_init)
    outs = []
    for blk in range(S // Q_BLOCK):
        start = blk * Q_BLOCK
        end = start + Q_BLOCK
        s = jnp.einsum('bqhcd,bkhcd->bhcqk', q[:, start:end], k[:, :end])
        qpos = start + jnp.arange(Q_BLOCK)
        kpos = jnp.arange(end)
        s = jnp.where(kpos[None, :] <= qpos[:, None], s, -jnp.inf)
        p = jax.nn.softmax(s, axis=-1)
        attn = p[:, :, 0] - lam * p[:, :, 1]
        outs.append(jnp.einsum('bhqk,bkhd->bqhd', attn, v[:, :end]))
    o = jnp.concatenate(outs, axis=1)
    o = rms_norm(o, subln_w) * (1.0 - lambda_init)
    return o.reshape(B, S, DIFF_WIDTH)


def setup_inputs(seed: int = 0) -> dict:
    key = jax.random.key(seed)
    ks = jax.random.split(key, 20)
    nrm = jax.random.normal
    f32 = jnp.float32
    dt = jnp.exp(jax.random.uniform(ks[5], (DEPTH, GDN_HEADS), f32, math.log(1e-3), math.log(1e-1)))
    return {
        "x": nrm(ks[0], (BATCH, SEQ, D_MODEL), f32),
        "norm1_w": 1.0 + 0.02 * nrm(ks[1], (DEPTH, D_MODEL), f32),
        "w_in": nrm(ks[2], (DEPTH, D_MODEL, D_IN), f32) * D_MODEL ** -0.5,
        "conv_w": nrm(ks[3], (DEPTH, CONV_K, 3 * GDN_WIDTH), f32) * CONV_K ** -0.5,
        "a_log": jnp.log(jax.random.uniform(ks[4], (DEPTH, GDN_HEADS), f32, 1.0, 16.0)),
        "dt_bias": dt + jnp.log(-jnp.expm1(-dt)),
        "gdn_norm_w": 1.0 + 0.02 * nrm(ks[6], (DEPTH, GDN_HEAD_DIM), f32),
        "q_norm_w": 1.0 + 0.02 * nrm(ks[7], (DEPTH, DIFF_HEAD_DIM), f32),
        "k_norm_w": 1.0 + 0.02 * nrm(ks[8], (DEPTH, DIFF_HEAD_DIM), f32),
        "lambda_q1": 0.1 * nrm(ks[9], (DEPTH, DIFF_HEAD_DIM), f32),
        "lambda_k1": 0.1 * nrm(ks[10], (DEPTH, DIFF_HEAD_DIM), f32),
        "lambda_q2": 0.1 * nrm(ks[11], (DEPTH, DIFF_HEAD_DIM), f32),
        "lambda_k2": 0.1 * nrm(ks[12], (DEPTH, DIFF_HEAD_DIM), f32),
        "subln_w": 1.0 + 0.02 * nrm(ks[13], (DEPTH, DIFF_V_DIM), f32),
        "w_out": nrm(ks[14], (DEPTH, D_MODEL, D_MODEL), f32) * D_MODEL ** -0.5,
        "norm2_w": 1.0 + 0.02 * nrm(ks[15], (DEPTH, D_MODEL), f32),
        "w_gate": nrm(ks[16], (DEPTH, D_MODEL, D_FF), f32) * D_MODEL ** -0.5,
        "w_up": nrm(ks[17], (DEPTH, D_MODEL, D_FF), f32) * D_MODEL ** -0.5,
        "w_down": nrm(ks[18], (DEPTH, D_FF, D_MODEL), f32) * D_FF ** -0.5,
    }


def reference(x, norm1_w, w_in, conv_w, a_log, dt_bias, gdn_norm_w, q_norm_w, k_norm_w,
              lambda_q1, lambda_k1, lambda_q2, lambda_k2, subln_w, w_out, norm2_w,
              w_gate, w_up, w_down):
    h = x
    for l in range(DEPTH):
        lambda_init = 0.8 - 0.6 * math.exp(-0.3 * l)
        u = rms_norm(h, norm1_w[l])
        proj = u @ w_in[l]
        (gq, gk, gv, gz, ga, gb, dq, dk, dv, gate_a, gate_b) = jnp.split(proj, SPLIT_POINTS, axis=-1)
        o_a = gated_delta_net(gq, gk, gv, gz, ga, gb, conv_w[l], a_log[l], dt_bias[l],
                              gdn_norm_w[l]).astype(h.dtype)
        o_b = diff_attention(dq, dk, dv, q_norm_w[l], k_norm_w[l], lambda_q1[l], lambda_k1[l],
                             lambda_q2[l], lambda_k2[l], subln_w[l], lambda_init).astype(h.dtype)
        mixed = jax.nn.sigmoid(gate_a) * o_a + jax.nn.sigmoid(gate_b) * o_b
        h = h + mixed @ w_out[l]
        u = rms_norm(h, norm2_w[l])
        h = h + (jax.nn.silu(u @ w_gate[l]) * (u @ w_up[l])) @ w_down[l]
    return h
```

```python
import functools
import math

import jax
import jax.numpy as jnp
from jax import lax
from jax.experimental import pallas as pl
from jax.experimental.pallas import tpu as pltpu

F32 = jnp.float32
BF16 = jnp.bfloat16

EPS = 1e-6
CONV_K = 4
GDN_HEAD_DIM = 128
DIFF_HEAD_DIM = 64
LAMBDA_INIT = 0.8 - 0.6 * math.exp(-0.3 * 0)

LANES = 128
SUBLANES = 8
VMEM_LIMIT = 56 * 1024 * 1024

SEG_GQ, SEG_GK, SEG_GV, SEG_GZ, SEG_DQ, SEG_DK, SEG_DV, SEG_GATE_A, SEG_GATE_B = range(9)
NUM_SEG = 9

GDN_CHUNK = 128
GDN_ROWS = 512
ATTN_BLOCK = 256


def _bdot(a, b):
    return jnp.dot(a.astype(BF16), b.astype(BF16), preferred_element_type=F32)


def _bdot_nt(a, b):
    return lax.dot_general(a.astype(BF16), b.astype(BF16), (((1,), (1,)), ((), ())),
                           preferred_element_type=F32)


def _bdot_tn(a, b):
    return lax.dot_general(a.astype(BF16), b.astype(BF16), (((0,), (0,)), ((), ())),
                           preferred_element_type=F32)


def _sigmoid(x):
    return 1.0 / (1.0 + jnp.exp(-x))


def _inproj_kernel(x_ref, n1_ref, w_ref, wab_ref, proj_ref, ab_ref, xn_ref):
    j = pl.program_id(1)

    @pl.when(j == 0)
    def _():
        x = x_ref[...]
        ms = jnp.mean(x * x, axis=-1, keepdims=True)
        xn_ref[...] = (x * lax.rsqrt(ms + EPS) * n1_ref[...]).astype(BF16)
        ab_ref[...] = jnp.dot(xn_ref[...], wab_ref[...], preferred_element_type=F32)

    y = jnp.dot(xn_ref[...], w_ref[...], preferred_element_type=F32)

    @pl.when(j < SEG_GATE_A)
    def _():
        proj_ref[...] = y.astype(BF16)

    @pl.when(j >= SEG_GATE_A)
    def _():
        proj_ref[...] = _sigmoid(y).astype(BF16)


def _inproj(x2, n1, w_main, w_ab, tm):
    t, d = x2.shape
    return pl.pallas_call(
        _inproj_kernel,
        grid=(t // tm, NUM_SEG),
        in_specs=[
            pl.BlockSpec((tm, d), lambda i, j: (i, 0)),
            pl.BlockSpec((1, d), lambda i, j: (0, 0)),
            pl.BlockSpec((d, d), lambda i, j: (0, j)),
            pl.BlockSpec((d, LANES), lambda i, j: (0, 0)),
        ],
        out_specs=[
            pl.BlockSpec((tm, d), lambda i, j: (i, j)),
            pl.BlockSpec((tm, LANES), lambda i, j: (i, 0)),
        ],
        out_shape=[
            jax.ShapeDtypeStruct((t, NUM_SEG * d), BF16),
            jax.ShapeDtypeStruct((t, LANES), F32),
        ],
        scratch_shapes=[pltpu.VMEM((tm, d), BF16)],
        compiler_params=pltpu.CompilerParams(
            dimension_semantics=("parallel", "arbitrary"), vmem_limit_bytes=VMEM_LIMIT),
        name="inproj",
    )(x2, n1, w_main, w_ab)


def _split3(x):
    h = x.astype(BF16)
    r = x - h.astype(F32)
    m = r.astype(BF16)
    l = (r - m.astype(F32)).astype(BF16)
    return h, m, l


def _tri_inverse(a, eye, n_doublings):
    p = -a
    t = eye + p
    for _ in range(n_doublings):
        p = _bdot(p, p)
        t = t + _bdot(t, p)
    return t


def _gdn_kernel(q_ref, k_ref, v_ref, z_ref, ga_ref, ab_ref, cw_ref, hv_ref, nw_ref, o_ref,
                xbuf, carry, state, qkv, *, rows, chunk, heads):
    s = pl.program_id(1)
    width = heads * GDN_HEAD_DIM
    pad = SUBLANES

    @pl.when(s == 0)
    def _():
        carry[...] = jnp.zeros_like(carry)
        state[...] = jnp.zeros_like(state)

    xbuf[0:pad, :] = carry[...]
    xbuf[pad:pad + rows, 0:width] = q_ref[...].astype(F32)
    xbuf[pad:pad + rows, width:2 * width] = k_ref[...].astype(F32)
    xbuf[pad:pad + rows, 2 * width:3 * width] = v_ref[...].astype(F32)
    carry[...] = xbuf[rows:rows + pad, :]

    ri = lax.broadcasted_iota(jnp.int32, (chunk, chunk), 0)
    ci = lax.broadcasted_iota(jnp.int32, (chunk, chunk), 1)
    lower = ri >= ci
    strict = ri > ci
    eye = jnp.where(ri == ci, 1.0, 0.0).astype(F32)
    tri_ones = jnp.where(lower, 1.0, 0.0).astype(BF16)
    n_doublings = int(math.log2(chunk)) - 1
    a_neg = -jnp.exp(hv_ref[0:1, :])
    dt_bias = hv_ref[1:2, :]
    norm_w = nw_ref[...]

    def l2n(x):
        return x * lax.rsqrt(jnp.sum(x * x, axis=-1, keepdims=True) + EPS)

    for part in range(3):
        for h in range(heads):
            col = part * width + h * GDN_HEAD_DIM
            for rc in range(rows // chunk):
                acc = None
                for j in range(CONV_K):
                    start = rc * chunk + (pad - CONV_K + 1) + j
                    term = (xbuf[start:start + chunk, col:col + GDN_HEAD_DIM]
                            * cw_ref[j:j + 1, col:col + GDN_HEAD_DIM])
                    acc = term if acc is None else acc + term
                y = acc * _sigmoid(acc)
                if part == 0:
                    y = l2n(y) * (GDN_HEAD_DIM ** -0.5)
                elif part == 1:
                    y = l2n(y)
                qkv[rc * chunk:(rc + 1) * chunk, col:col + GDN_HEAD_DIM] = y

    def chunk_body(c, carry_unused):
        r0 = pl.multiple_of(c * chunk, chunk)
        ab = ab_ref[pl.ds(r0, chunk), :]
        xa = ab + dt_bias
        softplus = jnp.maximum(xa, 0.0) + jnp.log1p(jnp.exp(-jnp.abs(xa)))
        g = a_neg * softplus
        beta = _sigmoid(ab)
        g1, g2, g3 = _split3(g)
        gc = (jnp.dot(tri_ones, g1, preferred_element_type=F32)
              + jnp.dot(tri_ones, g2, preferred_element_type=F32)
              + jnp.dot(tri_ones, g3, preferred_element_type=F32))
        gct = gc.T

        for h in range(heads):
            col = h * GDN_HEAD_DIM
            q = qkv[pl.ds(r0, chunk), col:col + GDN_HEAD_DIM]
            k = qkv[pl.ds(r0, chunk), width + col:width + col + GDN_HEAD_DIM]
            v = qkv[pl.ds(r0, chunk), 2 * width + col:2 * width + col + GDN_HEAD_DIM]

            gcol = jnp.broadcast_to(gc[:, h:h + 1], (chunk, GDN_HEAD_DIM))
            bcol = jnp.broadcast_to(beta[:, heads + h:heads + h + 1], (chunk, GDN_HEAD_DIM))
            grow = jnp.broadcast_to(gct[h:h + 1, :], (chunk, chunk))
            glast = jnp.broadcast_to(gc[chunk - 1:chunk, h:h + 1], (chunk, GDN_HEAD_DIM))

            decay = jnp.exp(jnp.where(lower, gcol - grow, 0.0))
            kk = _bdot_nt(k, k)
            qk = _bdot_nt(q, k)
            a_mat = jnp.where(strict, bcol * kk * decay, 0.0)
            a_qk = jnp.where(lower, qk * decay, 0.0)
            t_inv = _tri_inverse(a_mat, eye, n_doublings)

            eg = jnp.exp(gcol)
            kb = k * bcol
            uw = _bdot(t_inv, jnp.concatenate([v * bcol, kb * eg], axis=1))
            u = uw[:, :GDN_HEAD_DIM]
            w = uw[:, GDN_HEAD_DIM:]
            q_dec = q * eg
            k_dec = k * jnp.exp(glast - gcol)

            st = state[h]
            ws = _bdot(jnp.concatenate([w, q_dec], axis=0), st)
            v_new = u - ws[:chunk]
            o = ws[chunk:] + _bdot(a_qk, v_new)
            state[h] = st * jnp.exp(glast) + _bdot_tn(k_dec, v_new)

            o = o * lax.rsqrt(jnp.mean(o * o, axis=-1, keepdims=True) + EPS) * norm_w
            z = z_ref[pl.ds(r0, chunk), col:col + GDN_HEAD_DIM].astype(F32)
            gate = ga_ref[pl.ds(r0, chunk), col:col + GDN_HEAD_DIM].astype(F32)
            o_ref[pl.ds(r0, chunk), col:col + GDN_HEAD_DIM] = (
                o * (z * _sigmoid(z)) * gate).astype(BF16)
        return carry_unused

    lax.fori_loop(0, rows // chunk, chunk_body, 0)


def _gdn(proj, ab, conv_w, head_vecs, norm_w, batch, seq, heads):
    t = proj.shape[0]
    width = heads * GDN_HEAD_DIM
    rows = GDN_ROWS
    nsb = seq // rows

    def seg_spec(seg):
        return pl.BlockSpec((rows, width), lambda b, s, seg=seg: (b * nsb + s, seg))

    kern = functools.partial(_gdn_kernel, rows=rows, chunk=GDN_CHUNK, heads=heads)
    return pl.pallas_call(
        kern,
        grid=(batch, nsb),
        in_specs=[
            seg_spec(SEG_GQ), seg_spec(SEG_GK), seg_spec(SEG_GV), seg_spec(SEG_GZ),
            seg_spec(SEG_GATE_A),
            pl.BlockSpec((rows, LANES), lambda b, s: (b * nsb + s, 0)),
            pl.BlockSpec((CONV_K, 3 * width), lambda b, s: (0, 0)),
            pl.BlockSpec((SUBLANES, LANES), lambda b, s: (0, 0)),
            pl.BlockSpec((1, GDN_HEAD_DIM), lambda b, s: (0, 0)),
        ],
        out_specs=pl.BlockSpec((rows, width), lambda b, s: (b * nsb + s, 0)),
        out_shape=jax.ShapeDtypeStruct((t, width), BF16),
        scratch_shapes=[
            pltpu.VMEM((rows + SUBLANES, 3 * width), F32),
            pltpu.VMEM((SUBLANES, 3 * width), F32),
            pltpu.VMEM((heads, GDN_HEAD_DIM, GDN_HEAD_DIM), F32),
            pltpu.VMEM((rows, 3 * width), F32),
        ],
        compiler_params=pltpu.CompilerParams(
            dimension_semantics=("parallel", "arbitrary"), vmem_limit_bytes=VMEM_LIMIT),
        name="gdn",
    )(proj, proj, proj, proj, proj, ab, conv_w, head_vecs, norm_w)


def _half_rms(x, w, lo):
    x2 = x * x
    s_lo = jnp.sum(jnp.where(lo, x2, 0.0), axis=-1, keepdims=True)
    s_hi = jnp.sum(jnp.where(lo, 0.0, x2), axis=-1, keepdims=True)
    ms = jnp.where(lo, s_lo, s_hi) * (1.0 / DIFF_HEAD_DIM)
    return x * lax.rsqrt(ms + EPS) * w


def _attn_kernel(q_ref, k_ref, v_ref, gate_ref, qw_ref, kw_ref, lam_ref, sw_ref, o_ref, kn_ref,
                 *, blk, seq):
    qi = pl.program_id(2)
    lane = lax.broadcasted_iota(jnp.int32, (blk, 2 * DIFF_HEAD_DIM), 1)
    lo = lane < DIFF_HEAD_DIM

    @pl.when(qi == 0)
    def _():
        def norm_body(i, c):
            r0 = pl.multiple_of(i * blk, blk)
            kx = k_ref[pl.ds(r0, blk), :].astype(F32)
            kn_ref[pl.ds(r0, blk), :] = _half_rms(kx, kw_ref[...], lo).astype(BF16)
            return c
        lax.fori_loop(0, seq // blk, norm_body, 0)

    q = _half_rms(q_ref[...].astype(F32), qw_ref[...], lo) * (DIFF_HEAD_DIM ** -0.5)
    qs = jnp.concatenate([jnp.where(lo, q, 0.0), jnp.where(lo, 0.0, q)], axis=0).astype(BF16)

    def step(j, carry, masked):
        m, l, acc = carry
        r0 = pl.multiple_of(j * blk, blk)
        kb = kn_ref[pl.ds(r0, blk), :]
        vb = v_ref[pl.ds(r0, blk), :]
        sc = lax.dot_general(qs, kb, (((1,), (1,)), ((), ())), preferred_element_type=F32)
        if masked:
            rr = lax.broadcasted_iota(jnp.int32, (2 * blk, blk), 0)
            cc = lax.broadcasted_iota(jnp.int32, (2 * blk, blk), 1)
            rr = jnp.where(rr >= blk, rr - blk, rr)
            sc = jnp.where(cc <= rr, sc, -jnp.inf)
        m_new = jnp.maximum(m, jnp.max(sc, axis=-1, keepdims=True))
        alpha = jnp.exp(m - m_new)
        p = jnp.exp(sc - m_new)
        l_new = alpha * l + jnp.sum(p, axis=-1, keepdims=True)
        acc_new = alpha * acc + jnp.dot(p.astype(BF16), vb, preferred_element_type=F32)
        return m_new, l_new, acc_new

    init = (jnp.full((2 * blk, 1), -jnp.inf, F32), jnp.zeros((2 * blk, 1), F32),
            jnp.zeros((2 * blk, 2 * DIFF_HEAD_DIM), F32))
    carry = lax.fori_loop(0, qi, lambda j, c: step(j, c, False), init)
    m, l, acc = step(qi, carry, True)

    lp = lam_ref[...]
    lam = (jnp.exp(jnp.sum(lp[0:1, :] * lp[1:2, :], axis=-1, keepdims=True))
           - jnp.exp(jnp.sum(lp[2:3, :] * lp[3:4, :], axis=-1, keepdims=True)) + LAMBDA_INIT)
    on = acc / l
    o = on[:blk] - lam * on[blk:]
    o = o * lax.rsqrt(jnp.mean(o * o, axis=-1, keepdims=True) + EPS) * sw_ref[...]
    o = o * (1.0 - LAMBDA_INIT)
    o_ref[...] = (o * gate_ref[...].astype(F32)).astype(BF16)


def _attn(proj, qw, kw, lam, sw, batch, seq, heads):
    t = proj.shape[0]
    blk = ATTN_BLOCK
    hd = 2 * DIFF_HEAD_DIM
    nq = seq // blk
    kern = functools.partial(_attn_kernel, blk=blk, seq=seq)
    return pl.pallas_call(
        kern,
        grid=(batch, heads, nq),
        in_specs=[
            pl.BlockSpec((blk, hd), lambda b, h, i: (b * nq + i, SEG_DQ * heads + h)),
            pl.BlockSpec((seq, hd), lambda b, h, i: (b, SEG_DK * heads + h)),
            pl.BlockSpec((seq, hd), lambda b, h, i: (b, SEG_DV * heads + h)),
            pl.BlockSpec((blk, hd), lambda b, h, i: (b * nq + i, SEG_GATE_B * heads + h)),
            pl.BlockSpec((1, hd), lambda b, h, i: (0, 0)),
            pl.BlockSpec((1, hd), lambda b, h, i: (0, 0)),
            pl.BlockSpec((SUBLANES, hd), lambda b, h, i: (0, 0)),
            pl.BlockSpec((1, hd), lambda b, h, i: (0, 0)),
        ],
        out_specs=pl.BlockSpec((blk, hd), lambda b, h, i: (b * nq + i, h)),
        out_shape=jax.ShapeDtypeStruct((t, heads * hd), BF16),
        scratch_shapes=[pltpu.VMEM((seq, hd), BF16)],
        compiler_params=pltpu.CompilerParams(
            dimension_semantics=("parallel", "parallel", "arbitrary"),
            vmem_limit_bytes=VMEM_LIMIT),
        name="diffattn",
    )(proj, proj, proj, proj, qw, kw, lam, sw)


def _outproj_kernel(oa_ref, ob_ref, x_ref, wo_ref, n2_ref, h_ref, u_ref):
    mixed = (oa_ref[...].astype(F32) + ob_ref[...].astype(F32)).astype(BF16)
    h = x_ref[...] + jnp.dot(mixed, wo_ref[...], preferred_element_type=F32)
    h_ref[...] = h
    ms = jnp.mean(h * h, axis=-1, keepdims=True)
    u_ref[...] = (h * lax.rsqrt(ms + EPS) * n2_ref[...]).astype(BF16)


def _outproj(oa, ob, x2, w_out, n2, tm):
    t, d = x2.shape
    row = lambda i: (i, 0)
    fixed = lambda i: (0, 0)
    return pl.pallas_call(
        _outproj_kernel,
        grid=(t // tm,),
        in_specs=[
            pl.BlockSpec((tm, d), row), pl.BlockSpec((tm, d), row), pl.BlockSpec((tm, d), row),
            pl.BlockSpec((d, d), fixed), pl.BlockSpec((1, d), fixed),
        ],
        out_specs=[pl.BlockSpec((tm, d), row), pl.BlockSpec((tm, d), row)],
        out_shape=[jax.ShapeDtypeStruct((t, d), F32), jax.ShapeDtypeStruct((t, d), BF16)],
        compiler_params=pltpu.CompilerParams(
            dimension_semantics=("parallel",), vmem_limit_bytes=VMEM_LIMIT),
        name="outproj",
    )(oa, ob, x2, w_out, n2)


def _ffn_kernel(u_ref, h_ref, wg_ref, wu_ref, wd_ref, o_ref):
    f = pl.program_id(1)

    @pl.when(f == 0)
    def _():
        o_ref[...] = h_ref[...]

    u = u_ref[...]
    g = jnp.dot(u, wg_ref[...], preferred_element_type=F32)
    up = jnp.dot(u, wu_ref[...], preferred_element_type=F32)
    act = (g * _sigmoid(g) * up).astype(BF16)
    o_ref[...] += jnp.dot(act, wd_ref[...], preferred_element_type=F32)


def _ffn(u, h, wg, wu, wd, tm, tf):
    t, d = h.shape
    dff = wg.shape[1]
    return pl.pallas_call(
        _ffn_kernel,
        grid=(t // tm, dff // tf),
        in_specs=[
            pl.BlockSpec((tm, d), lambda i, f: (i, 0)),
            pl.BlockSpec((tm, d), lambda i, f: (i, 0)),
            pl.BlockSpec((d, tf), lambda i, f: (0, f)),
            pl.BlockSpec((d, tf), lambda i, f: (0, f)),
            pl.BlockSpec((tf, d), lambda i, f: (f, 0)),
        ],
        out_specs=pl.BlockSpec((tm, d), lambda i, f: (i, 0)),
        out_shape=jax.ShapeDtypeStruct((t, d), F32),
        compiler_params=pltpu.CompilerParams(
            dimension_semantics=("parallel", "arbitrary"), vmem_limit_bytes=VMEM_LIMIT),
        name="ffn",
    )(u, h, wg, wu, wd)


def _layer(h2, batch, seq, norm1_w, w_in, conv_w, a_log, dt_bias, gdn_norm_w, q_norm_w, k_norm_w,
           lambda_q1, lambda_k1, lambda_q2, lambda_k2, subln_w, w_out, norm2_w, w_gate, w_up,
           w_down):
    d = h2.shape[1]
    heads = d // GDN_HEAD_DIM
    c0 = 4 * d
    c1 = c0 + 2 * heads
    w_main = jnp.concatenate([w_in[:, :c0], w_in[:, c1:]], axis=1).astype(BF16)
    w_ab = jnp.pad(w_in[:, c0:c1], ((0, 0), (0, LANES - 2 * heads))).astype(BF16)
    head_vecs = jnp.zeros((SUBLANES, LANES), F32)
    head_vecs = head_vecs.at[0, :heads].set(a_log).at[1, :heads].set(dt_bias)
    lam = jnp.zeros((SUBLANES, 2 * DIFF_HEAD_DIM), F32)
    lam = (lam.at[0, :DIFF_HEAD_DIM].set(lambda_q1).at[1, :DIFF_HEAD_DIM].set(lambda_k1)
           .at[2, :DIFF_HEAD_DIM].set(lambda_q2).at[3, :DIFF_HEAD_DIM].set(lambda_k2))
    qw = jnp.tile(q_norm_w, 2)[None, :]
    kw = jnp.tile(k_norm_w, 2)[None, :]

    proj, ab = _inproj(h2, norm1_w[None, :], w_main, w_ab, tm=1024)
    oa = _gdn(proj, ab, conv_w, head_vecs, gdn_norm_w[None, :], batch, seq, heads)
    ob = _attn(proj, qw, kw, lam, subln_w[None, :], batch, seq, heads)
    h1, u2 = _outproj(oa, ob, h2, w_out.astype(BF16), norm2_w[None, :], tm=1024)
    return _ffn(u2, h1, w_gate.astype(BF16), w_up.astype(BF16), w_down.astype(BF16),
                tm=1024, tf=256)


def kernel(x, norm1_w, w_in, conv_w, a_log, dt_bias, gdn_norm_w, q_norm_w, k_norm_w, lambda_q1,
           lambda_k1, lambda_q2, lambda_k2, subln_w, w_out, norm2_w, w_gate, w_up, w_down):
    batch, seq, d = x.shape
    depth = norm1_w.shape[0]
    assert depth == 1, "lambda_init is specialised to a single layer"
    h2 = x.reshape(batch * seq, d)
    for l in range(depth):
        h2 = _layer(h2, batch, seq, norm1_w[l], w_in[l], conv_w[l], a_log[l], dt_bias[l],
                    gdn_norm_w[l], q_norm_w[l], k_norm_w[l], lambda_q1[l], lambda_k1[l],
                    lambda_q2[l], lambda_k2[l], subln_w[l], w_out[l], norm2_w[l], w_gate[l],
                    w_up[l], w_down[l])
    return h2.reshape(batch, seq, d)
```

```python
import functools
import math

import jax
import jax.numpy as jnp
from jax import lax
from jax.experimental import pallas as pl
from jax.experimental.pallas import tpu as pltpu

F32 = jnp.float32
BF16 = jnp.bfloat16

EPS = 1e-6
CONV_K = 4
GDN_HEAD_DIM = 128
DIFF_HEAD_DIM = 64
LAMBDA_INIT = 0.8 - 0.6 * math.exp(-0.3 * 0)

LANES = 128
SUBLANES = 8
VMEM_LIMIT = 56 * 1024 * 1024

SEG_GQ, SEG_GK, SEG_GV, SEG_GZ, SEG_DQ, SEG_DK, SEG_DV, SEG_GATE_A, SEG_GATE_B = range(9)
NUM_SEG = 9

GDN_CHUNK = 128
GDN_ROWS = 512
ATTN_BLOCK = 256


def _bdot(a, b):
    return jnp.dot(a.astype(BF16), b.astype(BF16), preferred_element_type=F32)


def _bdot_nt(a, b):
    return lax.dot_general(a.astype(BF16), b.astype(BF16), (((1,), (1,)), ((), ())),
                           preferred_element_type=F32)


def _bdot_tn(a, b):
    return lax.dot_general(a.astype(BF16), b.astype(BF16), (((0,), (0,)), ((), ())),
                           preferred_element_type=F32)


def _sigmoid(x):
    return 1.0 / (1.0 + jnp.exp(-x))


def _inproj_kernel(x_ref, n1_ref, w_ref, wab_ref, proj_ref, ab_ref, xn_ref):
    j = pl.program_id(1)

    @pl.when(j == 0)
    def _():
        x = x_ref[...]
        ms = jnp.mean(x * x, axis=-1, keepdims=True)
        xn_ref[...] = (x * lax.rsqrt(ms + EPS) * n1_ref[...]).astype(BF16)
        ab_ref[...] = jnp.dot(xn_ref[...], wab_ref[...], preferred_element_type=F32)

    y = jnp.dot(xn_ref[...], w_ref[...], preferred_element_type=F32)

    @pl.when(j < SEG_GATE_A)
    def _():
        proj_ref[...] = y.astype(BF16)

    @pl.when(j >= SEG_GATE_A)
    def _():
        proj_ref[...] = _sigmoid(y).astype(BF16)


def _inproj(x2, n1, w_main, w_ab, tm):
    t, d = x2.shape
    return pl.pallas_call(
        _inproj_kernel,
        grid=(t // tm, NUM_SEG),
        in_specs=[
            pl.BlockSpec((tm, d), lambda i, j: (i, 0)),
            pl.BlockSpec((1, d), lambda i, j: (0, 0)),
            pl.BlockSpec((d, d), lambda i, j: (0, j)),
            pl.BlockSpec((d, LANES), lambda i, j: (0, 0)),
        ],
        out_specs=[
            pl.BlockSpec((tm, d), lambda i, j: (i, j)),
            pl.BlockSpec((tm, LANES), lambda i, j: (i, 0)),
        ],
        out_shape=[
            jax.ShapeDtypeStruct((t, NUM_SEG * d), BF16),
            jax.ShapeDtypeStruct((t, LANES), F32),
        ],
        scratch_shapes=[pltpu.VMEM((tm, d), BF16)],
        compiler_params=pltpu.CompilerParams(
            dimension_semantics=("parallel", "arbitrary"), vmem_limit_bytes=VMEM_LIMIT),
        name="inproj",
    )(x2, n1, w_main, w_ab)


def _split3(x):
    h = x.astype(BF16)
    r = x - h.astype(F32)
    m = r.astype(BF16)
    l = (r - m.astype(F32)).astype(BF16)
    return h, m, l


def _gdn_kernel(q_ref, k_ref, v_ref, z_ref, ga_ref, ab_ref, cw_ref, hv_ref, nw_ref, o_ref,
                xbuf, carry, state, qkv, *, rows, chunk, heads):
    s = pl.program_id(1)
    width = heads * GDN_HEAD_DIM
    pad = SUBLANES

    @pl.when(s == 0)
    def _():
        carry[...] = jnp.zeros_like(carry)
        state[...] = jnp.zeros_like(state)

    xbuf[0:pad, :] = carry[...]
    xbuf[pad:pad + rows, 0:width] = q_ref[...].astype(F32)
    xbuf[pad:pad + rows, width:2 * width] = k_ref[...].astype(F32)
    xbuf[pad:pad + rows, 2 * width:3 * width] = v_ref[...].astype(F32)
    carry[...] = xbuf[rows:rows + pad, :]

    ri = lax.broadcasted_iota(jnp.int32, (chunk, chunk), 0)
    ci = lax.broadcasted_iota(jnp.int32, (chunk, chunk), 1)
    lower = ri >= ci
    strict = ri > ci
    eye = jnp.where(ri == ci, 1.0, 0.0).astype(F32)
    tri_ones = jnp.where(lower, 1.0, 0.0).astype(BF16)
    n_doublings = int(math.log2(chunk)) - 1
    a_neg = -jnp.exp(hv_ref[0:1, :])
    dt_bias = hv_ref[1:2, :]
    norm_w = nw_ref[...]

    def l2n(x):
        return x * lax.rsqrt(jnp.sum(x * x, axis=-1, keepdims=True) + EPS)

    for part in range(3):
        for h in range(heads):
            col = part * width + h * GDN_HEAD_DIM
            for rc in range(rows // chunk):
                acc = None
                for j in range(CONV_K):
                    start = rc * chunk + (pad - CONV_K + 1) + j
                    term = (xbuf[start:start + chunk, col:col + GDN_HEAD_DIM]
                            * cw_ref[j:j + 1, col:col + GDN_HEAD_DIM])
                    acc = term if acc is None else acc + term
                y = acc * _sigmoid(acc)
                if part == 0:
                    y = l2n(y) * (GDN_HEAD_DIM ** -0.5)
                elif part == 1:
                    y = l2n(y)
                qkv[rc * chunk:(rc + 1) * chunk, col:col + GDN_HEAD_DIM] = y

    def chunk_body(c, carry_unused):
        r0 = pl.multiple_of(c * chunk, chunk)
        ab = ab_ref[pl.ds(r0, chunk), :]
        xa = ab + dt_bias
        softplus = jnp.maximum(xa, 0.0) + jnp.log1p(jnp.exp(-jnp.abs(xa)))
        g = a_neg * softplus
        beta = _sigmoid(ab)
        g1, g2, g3 = _split3(g)
        gc = (jnp.dot(tri_ones, g1, preferred_element_type=F32)
              + jnp.dot(tri_ones, g2, preferred_element_type=F32)
              + jnp.dot(tri_ones, g3, preferred_element_type=F32))
        gct = gc.T

        hs = range(heads)
        cols = [h * GDN_HEAD_DIM for h in hs]
        q = [qkv[pl.ds(r0, chunk), c0:c0 + GDN_HEAD_DIM] for c0 in cols]
        k = [qkv[pl.ds(r0, chunk), width + c0:width + c0 + GDN_HEAD_DIM] for c0 in cols]
        v = [qkv[pl.ds(r0, chunk), 2 * width + c0:2 * width + c0 + GDN_HEAD_DIM] for c0 in cols]
        gcol = [jnp.broadcast_to(gc[:, h:h + 1], (chunk, GDN_HEAD_DIM)) for h in hs]
        bcol = [jnp.broadcast_to(beta[:, heads + h:heads + h + 1], (chunk, GDN_HEAD_DIM))
                for h in hs]
        glast = [jnp.broadcast_to(gc[chunk - 1:chunk, h:h + 1], (chunk, GDN_HEAD_DIM))
                 for h in hs]
        decay = [jnp.exp(jnp.where(lower, gcol[h] - jnp.broadcast_to(gct[h:h + 1, :],
                                                                     (chunk, chunk)), 0.0))
                 for h in hs]
        gram = [_bdot_nt(jnp.concatenate([k[h], q[h]], axis=0), k[h]) for h in hs]
        a_qk = [jnp.where(lower, gram[h][chunk:] * decay[h], 0.0) for h in hs]
        p = [jnp.where(strict, -(bcol[h] * gram[h][:chunk] * decay[h]), 0.0) for h in hs]
        t_inv = [eye + p[h] for h in hs]
        for _ in range(n_doublings):
            p = [_bdot(p[h], p[h]) for h in hs]
            t_inv = [t_inv[h] + _bdot(t_inv[h], p[h]) for h in hs]

        eg = [jnp.exp(gcol[h]) for h in hs]
        uw = [_bdot(t_inv[h], jnp.concatenate([v[h] * bcol[h], k[h] * bcol[h] * eg[h]], axis=1))
              for h in hs]
        st = [state[h] for h in hs]
        ws = [_bdot(jnp.concatenate([uw[h][:, GDN_HEAD_DIM:], q[h] * eg[h]], axis=0), st[h])
              for h in hs]
        v_new = [uw[h][:, :GDN_HEAD_DIM] - ws[h][:chunk] for h in hs]
        o = [ws[h][chunk:] + _bdot(a_qk[h], v_new[h]) for h in hs]
        for h in hs:
            k_dec = k[h] * jnp.exp(glast[h] - gcol[h])
            state[h] = st[h] * jnp.exp(glast[h]) + _bdot_tn(k_dec, v_new[h])
        for h in hs:
            c0 = cols[h]
            on = o[h] * lax.rsqrt(jnp.mean(o[h] * o[h], axis=-1, keepdims=True) + EPS) * norm_w
            z = z_ref[pl.ds(r0, chunk), c0:c0 + GDN_HEAD_DIM].astype(F32)
            gate = ga_ref[pl.ds(r0, chunk), c0:c0 + GDN_HEAD_DIM].astype(F32)
            o_ref[pl.ds(r0, chunk), c0:c0 + GDN_HEAD_DIM] = (
                on * (z * _sigmoid(z)) * gate).astype(BF16)
        return carry_unused

    lax.fori_loop(0, rows // chunk, chunk_body, 0)


def _gdn(proj, ab, conv_w, head_vecs, norm_w, batch, seq, heads):
    t = proj.shape[0]
    width = heads * GDN_HEAD_DIM
    rows = GDN_ROWS
    nsb = seq // rows

    def seg_spec(seg):
        return pl.BlockSpec((rows, width), lambda b, s, seg=seg: (b * nsb + s, seg))

    kern = functools.partial(_gdn_kernel, rows=rows, chunk=GDN_CHUNK, heads=heads)
    return pl.pallas_call(
        kern,
        grid=(batch, nsb),
        in_specs=[
            seg_spec(SEG_GQ), seg_spec(SEG_GK), seg_spec(SEG_GV), seg_spec(SEG_GZ),
            seg_spec(SEG_GATE_A),
            pl.BlockSpec((rows, LANES), lambda b, s: (b * nsb + s, 0)),
            pl.BlockSpec((CONV_K, 3 * width), lambda b, s: (0, 0)),
            pl.BlockSpec((SUBLANES, LANES), lambda b, s: (0, 0)),
            pl.BlockSpec((1, GDN_HEAD_DIM), lambda b, s: (0, 0)),
        ],
        out_specs=pl.BlockSpec((rows, width), lambda b, s: (b * nsb + s, 0)),
        out_shape=jax.ShapeDtypeStruct((t, width), BF16),
        scratch_shapes=[
            pltpu.VMEM((rows + SUBLANES, 3 * width), F32),
            pltpu.VMEM((SUBLANES, 3 * width), F32),
            pltpu.VMEM((heads, GDN_HEAD_DIM, GDN_HEAD_DIM), F32),
            pltpu.VMEM((rows, 3 * width), F32),
        ],
        compiler_params=pltpu.CompilerParams(
            dimension_semantics=("parallel", "arbitrary"), vmem_limit_bytes=VMEM_LIMIT),
        name="gdn",
    )(proj, proj, proj, proj, proj, ab, conv_w, head_vecs, norm_w)


def _half_rms(x, w, lo):
    x2 = x * x
    s_lo = jnp.sum(jnp.where(lo, x2, 0.0), axis=-1, keepdims=True)
    s_hi = jnp.sum(jnp.where(lo, 0.0, x2), axis=-1, keepdims=True)
    ms = jnp.where(lo, s_lo, s_hi) * (1.0 / DIFF_HEAD_DIM)
    return x * lax.rsqrt(ms + EPS) * w


def _attn_kernel(q_ref, k_ref, v_ref, gate_ref, qw_ref, kw_ref, lam_ref, sw_ref, o_ref,
                 kn_ref, qs_ref, vt_ref, *, blk, seq):
    hd = 2 * DIFF_HEAD_DIM
    nblk = seq // blk
    lane = lax.broadcasted_iota(jnp.int32, (blk, hd), 1)
    lo = lane < DIFF_HEAD_DIM
    q_scale = (DIFF_HEAD_DIM ** -0.5) * math.log2(math.e)

    for i in range(nblk):
        rows = slice(i * blk, (i + 1) * blk)
        kn_ref[rows, :] = _half_rms(k_ref[rows, :].astype(F32), kw_ref[...], lo).astype(BF16)
        qn = _half_rms(q_ref[rows, :].astype(F32), qw_ref[...], lo) * q_scale
        qs_ref[0, rows, :] = jnp.where(lo, qn, 0.0).astype(BF16)
        qs_ref[1, rows, :] = jnp.where(lo, 0.0, qn).astype(BF16)
    for c in range(seq // hd):
        rows = slice(c * hd, (c + 1) * hd)
        vt_ref[:, rows] = v_ref[rows, :].astype(F32).T.astype(BF16)

    lp = lam_ref[...]
    lam = (jnp.exp(jnp.sum(lp[0:1, :] * lp[1:2, :], axis=-1, keepdims=True))
           - jnp.exp(jnp.sum(lp[2:3, :] * lp[3:4, :], axis=-1, keepdims=True)) + LAMBDA_INIT)
    kv_idx = lax.broadcasted_iota(jnp.int32, (blk, 2 * blk), 0)
    q_idx = lax.broadcasted_iota(jnp.int32, (blk, 2 * blk), 1)
    q_idx = jnp.where(q_idx >= blk, q_idx - blk, q_idx)
    causal = kv_idx <= q_idx

    def scores(i, j):
        qs = jnp.concatenate([qs_ref[0, i * blk:(i + 1) * blk, :],
                              qs_ref[1, i * blk:(i + 1) * blk, :]], axis=0)
        return lax.dot_general(kn_ref[j * blk:(j + 1) * blk, :], qs, (((1,), (1,)), ((), ())),
                               preferred_element_type=F32)

    pairs = [(i, j) for i in range(nblk) for j in range(i + 1)]
    sc_next = scores(*pairs[0])
    m = l = acc = None
    for n, (i, j) in enumerate(pairs):
        sc = sc_next
        if n + 1 < len(pairs):
            sc_next = scores(*pairs[n + 1])
        if j == i:
            sc = jnp.where(causal, sc, -jnp.inf)
        vt = vt_ref[:, j * blk:(j + 1) * blk]
        if j == 0:
            m = jnp.max(sc, axis=0, keepdims=True)
            p = jnp.exp2(sc - m)
            l = jnp.sum(p, axis=0, keepdims=True)
            acc = jnp.dot(vt, p.astype(BF16), preferred_element_type=F32)
        else:
            m_new = jnp.maximum(m, jnp.max(sc, axis=0, keepdims=True))
            alpha = jnp.exp2(m - m_new)
            p = jnp.exp2(sc - m_new)
            l = alpha * l + jnp.sum(p, axis=0, keepdims=True)
            acc = alpha * acc + jnp.dot(vt, p.astype(BF16), preferred_element_type=F32)
            m = m_new
        if j == i:
            rows = slice(i * blk, (i + 1) * blk)
            on = acc / l
            o = (on[:, :blk] - lam * on[:, blk:]).T
            o = o * lax.rsqrt(jnp.mean(o * o, axis=-1, keepdims=True) + EPS) * sw_ref[...]
            o = o * (1.0 - LAMBDA_INIT)
            o_ref[rows, :] = (o * gate_ref[rows, :].astype(F32)).astype(BF16)


def _attn(proj, qw, kw, lam, sw, batch, seq, heads):
    t = proj.shape[0]
    blk = ATTN_BLOCK
    hd = 2 * DIFF_HEAD_DIM
    kern = functools.partial(_attn_kernel, blk=blk, seq=seq)
    fixed = lambda b, h: (0, 0)
    return pl.pallas_call(
        kern,
        grid=(batch, heads),
        in_specs=[
            pl.BlockSpec((seq, hd), lambda b, h: (b, SEG_DQ * heads + h)),
            pl.BlockSpec((seq, hd), lambda b, h: (b, SEG_DK * heads + h)),
            pl.BlockSpec((seq, hd), lambda b, h: (b, SEG_DV * heads + h)),
            pl.BlockSpec((seq, hd), lambda b, h: (b, SEG_GATE_B * heads + h)),
            pl.BlockSpec((1, hd), fixed),
            pl.BlockSpec((1, hd), fixed),
            pl.BlockSpec((SUBLANES, hd), fixed),
            pl.BlockSpec((1, hd), fixed),
        ],
        out_specs=pl.BlockSpec((seq, hd), lambda b, h: (b, h)),
        out_shape=jax.ShapeDtypeStruct((t, heads * hd), BF16),
        scratch_shapes=[pltpu.VMEM((seq, hd), BF16), pltpu.VMEM((2, seq, hd), BF16),
                        pltpu.VMEM((hd, seq), BF16)],
        compiler_params=pltpu.CompilerParams(
            dimension_semantics=("parallel", "parallel"), vmem_limit_bytes=VMEM_LIMIT),
        name="diffattn",
    )(proj, proj, proj, proj, qw, kw, lam, sw)


def _outproj_kernel(oa_ref, ob_ref, x_ref, wo_ref, n2_ref, h_ref, u_ref):
    mixed = (oa_ref[...].astype(F32) + ob_ref[...].astype(F32)).astype(BF16)
    h = x_ref[...] + jnp.dot(mixed, wo_ref[...], preferred_element_type=F32)
    h_ref[...] = h
    ms = jnp.mean(h * h, axis=-1, keepdims=True)
    u_ref[...] = (h * lax.rsqrt(ms + EPS) * n2_ref[...]).astype(BF16)


def _outproj(oa, ob, x2, w_out, n2, tm):
    t, d = x2.shape
    row = lambda i: (i, 0)
    fixed = lambda i: (0, 0)
    return pl.pallas_call(
        _outproj_kernel,
        grid=(t // tm,),
        in_specs=[
            pl.BlockSpec((tm, d), row), pl.BlockSpec((tm, d), row), pl.BlockSpec((tm, d), row),
            pl.BlockSpec((d, d), fixed), pl.BlockSpec((1, d), fixed),
        ],
        out_specs=[pl.BlockSpec((tm, d), row), pl.BlockSpec((tm, d), row)],
        out_shape=[jax.ShapeDtypeStruct((t, d), F32), jax.ShapeDtypeStruct((t, d), BF16)],
        compiler_params=pltpu.CompilerParams(
            dimension_semantics=("parallel",), vmem_limit_bytes=VMEM_LIMIT),
        name="outproj",
    )(oa, ob, x2, w_out, n2)


def _ffn_kernel(u_ref, h_ref, wg_ref, wu_ref, wd_ref, o_ref):
    f = pl.program_id(1)

    @pl.when(f == 0)
    def _():
        o_ref[...] = h_ref[...]

    u = u_ref[...]
    g = jnp.dot(u, wg_ref[...], preferred_element_type=F32)
    up = jnp.dot(u, wu_ref[...], preferred_element_type=F32)
    act = (g * _sigmoid(g) * up).astype(BF16)
    o_ref[...] += jnp.dot(act, wd_ref[...], preferred_element_type=F32)


def _ffn(u, h, wg, wu, wd, tm, tf):
    t, d = h.shape
    dff = wg.shape[1]
    return pl.pallas_call(
        _ffn_kernel,
        grid=(t // tm, dff // tf),
        in_specs=[
            pl.BlockSpec((tm, d), lambda i, f: (i, 0)),
            pl.BlockSpec((tm, d), lambda i, f: (i, 0)),
            pl.BlockSpec((d, tf), lambda i, f: (0, f)),
            pl.BlockSpec((d, tf), lambda i, f: (0, f)),
            pl.BlockSpec((tf, d), lambda i, f: (f, 0)),
        ],
        out_specs=pl.BlockSpec((tm, d), lambda i, f: (i, 0)),
        out_shape=jax.ShapeDtypeStruct((t, d), F32),
        compiler_params=pltpu.CompilerParams(
            dimension_semantics=("parallel", "arbitrary"), vmem_limit_bytes=VMEM_LIMIT),
        name="ffn",
    )(u, h, wg, wu, wd)


def _layer(h2, batch, seq, norm1_w, w_in, conv_w, a_log, dt_bias, gdn_norm_w, q_norm_w, k_norm_w,
           lambda_q1, lambda_k1, lambda_q2, lambda_k2, subln_w, w_out, norm2_w, w_gate, w_up,
           w_down):
    d = h2.shape[1]
    heads = d // GDN_HEAD_DIM
    c0 = 4 * d
    c1 = c0 + 2 * heads
    w_main = jnp.concatenate([w_in[:, :c0], w_in[:, c1:]], axis=1).astype(BF16)
    w_ab = jnp.pad(w_in[:, c0:c1], ((0, 0), (0, LANES - 2 * heads))).astype(BF16)
    head_vecs = jnp.zeros((SUBLANES, LANES), F32)
    head_vecs = head_vecs.at[0, :heads].set(a_log).at[1, :heads].set(dt_bias)
    lam = jnp.zeros((SUBLANES, 2 * DIFF_HEAD_DIM), F32)
    lam = (lam.at[0, :DIFF_HEAD_DIM].set(lambda_q1).at[1, :DIFF_HEAD_DIM].set(lambda_k1)
           .at[2, :DIFF_HEAD_DIM].set(lambda_q2).at[3, :DIFF_HEAD_DIM].set(lambda_k2))
    qw = jnp.tile(q_norm_w, 2)[None, :]
    kw = jnp.tile(k_norm_w, 2)[None, :]

    proj, ab = _inproj(h2, norm1_w[None, :], w_main, w_ab, tm=1024)
    oa = _gdn(proj, ab, conv_w, head_vecs, gdn_norm_w[None, :], batch, seq, heads)
    ob = _attn(proj, qw, kw, lam, subln_w[None, :], batch, seq, heads)
    h1, u2 = _outproj(oa, ob, h2, w_out.astype(BF16), norm2_w[None, :], tm=1024)
    return _ffn(u2, h1, w_gate.astype(BF16), w_up.astype(BF16), w_down.astype(BF16),
                tm=1024, tf=256)


def kernel(x, norm1_w, w_in, conv_w, a_log, dt_bias, gdn_norm_w, q_norm_w, k_norm_w, lambda_q1,
           lambda_k1, lambda_q2, lambda_k2, subln_w, w_out, norm2_w, w_gate, w_up, w_down):
    batch, seq, d = x.shape
    depth = norm1_w.shape[0]
    assert depth == 1, "lambda_init is specialised to a single layer"
    h2 = x.reshape(batch * seq, d)
    for l in range(depth):
        h2 = _layer(h2, batch, seq, norm1_w[l], w_in[l], conv_w[l], a_log[l], dt_bias[l],
                    gdn_norm_w[l], q_norm_w[l], k_norm_w[l], lambda_q1[l], lambda_k1[l],
                    lambda_q2[l], lambda_k2[l], subln_w[l], w_out[l], norm2_w[l], w_gate[l],
                    w_up[l], w_down[l])
    return h2.reshape(batch, seq, d)
```

```python
import functools
import math

import jax
import jax.numpy as jnp
from jax import lax
from jax.experimental import pallas as pl
from jax.experimental.pallas import tpu as pltpu

F32 = jnp.float32
BF16 = jnp.bfloat16

EPS = 1e-6
CONV_K = 4
GDN_HEAD_DIM = 128
DIFF_HEAD_DIM = 64
LAMBDA_INIT = 0.8 - 0.6 * math.exp(-0.3 * 0)

LANES = 128
SUBLANES = 8
VMEM_LIMIT = 56 * 1024 * 1024

SEG_GQ, SEG_GK, SEG_GV, SEG_GZ, SEG_DQ, SEG_DK, SEG_DV, SEG_GATE_A, SEG_GATE_B = range(9)
NUM_SEG = 9

GDN_CHUNK = 128
GDN_ROWS = 1024
INPROJ_ROWS = 256
INPROJ_COLS = 256
ATTN_BLOCK = 256
MLP_ROWS = 256
MLP_COLS = 256


def _bdot(a, b):
    return jnp.dot(a.astype(BF16), b.astype(BF16), preferred_element_type=F32)


def _bdot_nt(a, b):
    return lax.dot_general(a.astype(BF16), b.astype(BF16), (((1,), (1,)), ((), ())),
                           preferred_element_type=F32)


def _bdot_tn(a, b):
    return lax.dot_general(a.astype(BF16), b.astype(BF16), (((0,), (0,)), ((), ())),
                           preferred_element_type=F32)


def _sigmoid(x):
    return 1.0 / (1.0 + jnp.exp(-x))


def _half_rms(x, w, lo):
    x2 = x * x
    s_lo = jnp.sum(jnp.where(lo, x2, 0.0), axis=-1, keepdims=True)
    s_hi = jnp.sum(jnp.where(lo, 0.0, x2), axis=-1, keepdims=True)
    ms = jnp.where(lo, s_lo, s_hi) * (1.0 / DIFF_HEAD_DIM)
    return x * lax.rsqrt(ms + EPS) * w


def _inproj_kernel(x_ref, n1_ref, w_ref, wab_ref, cw_ref, qw_ref, kw_ref, proj_ref, ab_ref,
                   xn_ref, ybuf, halo, *, tm, d, tiles_per_seq):
    i = pl.program_id(0)
    pad = SUBLANES
    x = x_ref[...]
    ms = jnp.mean(x * x, axis=-1, keepdims=True)
    xn_ref[...] = (x * lax.rsqrt(ms + EPS) * n1_ref[...]).astype(BF16)
    ab_ref[...] = jnp.dot(xn_ref[...], wab_ref[...], preferred_element_type=F32)
    seq_start = (i % tiles_per_seq) == 0
    lane = lax.broadcasted_iota(jnp.int32, (tm, LANES), 1)
    lo = lane < DIFF_HEAD_DIM
    q_scale = (DIFF_HEAD_DIM ** -0.5) * math.log2(math.e)

    def matmul(item):
        seg, cq = item
        col = seg * d + cq * INPROJ_COLS
        return jnp.dot(xn_ref[...], w_ref[:, col:col + INPROJ_COLS], preferred_element_type=F32)

    def epilogue(item, y):
        seg, cq = item
        lc = cq * INPROJ_COLS
        col = seg * d + lc
        if seg in (SEG_GQ, SEG_GK, SEG_GV):
            hist = jnp.where(seq_start, 0.0, halo[seg, :, lc:lc + INPROJ_COLS])
            ybuf[seg, 0:pad, lc:lc + INPROJ_COLS] = hist
            ybuf[seg, pad:pad + tm, lc:lc + INPROJ_COLS] = y
            halo[seg, :, lc:lc + INPROJ_COLS] = y[tm - pad:tm, :]
            for c0 in range(0, INPROJ_COLS, GDN_HEAD_DIM):
                for r0 in range(0, tm, GDN_CHUNK):
                    acc = None
                    for j in range(CONV_K):
                        start = r0 + (pad - CONV_K + 1) + j
                        term = (ybuf[seg, start:start + GDN_CHUNK, lc + c0:lc + c0 + GDN_HEAD_DIM]
                                * cw_ref[j:j + 1, col + c0:col + c0 + GDN_HEAD_DIM])
                        acc = term if acc is None else acc + term
                    v = acc * _sigmoid(acc)
                    if seg != SEG_GV:
                        v = v * lax.rsqrt(jnp.sum(v * v, axis=-1, keepdims=True) + EPS)
                    if seg == SEG_GQ:
                        v = v * (GDN_HEAD_DIM ** -0.5)
                    proj_ref[r0:r0 + GDN_CHUNK, col + c0:col + c0 + GDN_HEAD_DIM] = v.astype(BF16)
        elif seg == SEG_GZ:
            proj_ref[:, col:col + INPROJ_COLS] = (y * _sigmoid(y)).astype(BF16)
        elif seg in (SEG_DQ, SEG_DK):
            w_ref_n = qw_ref if seg == SEG_DQ else kw_ref
            for c0 in range(0, INPROJ_COLS, LANES):
                yn = _half_rms(y[:, c0:c0 + LANES], w_ref_n[...], lo)
                if seg == SEG_DQ:
                    yn = yn * q_scale
                proj_ref[:, col + c0:col + c0 + LANES] = yn.astype(BF16)
        elif seg == SEG_DV:
            proj_ref[:, col:col + INPROJ_COLS] = y.astype(BF16)
        else:
            proj_ref[:, col:col + INPROJ_COLS] = _sigmoid(y).astype(BF16)

    order = (SEG_GQ, SEG_DV, SEG_GK, SEG_GATE_A, SEG_GV, SEG_GATE_B, SEG_DQ, SEG_GZ, SEG_DK)
    items = [(seg, cq) for cq in range(d // INPROJ_COLS) for seg in order]
    y_next = matmul(items[0])
    for n, item in enumerate(items):
        y = y_next
        if n + 1 < len(items):
            y_next = matmul(items[n + 1])
        epilogue(item, y)


def _inproj(x2, n1, w_main, w_ab, conv_w, qw, kw, tm, seq):
    t, d = x2.shape
    fixed = lambda i: (0, 0)
    resident = pl.Buffered(1)
    kern = functools.partial(_inproj_kernel, tm=tm, d=d, tiles_per_seq=seq // tm)
    return pl.pallas_call(
        kern,
        grid=(t // tm,),
        in_specs=[
            pl.BlockSpec((tm, d), lambda i: (i, 0)),
            pl.BlockSpec((1, d), fixed),
            pl.BlockSpec((d, NUM_SEG * d), fixed, pipeline_mode=resident),
            pl.BlockSpec((d, LANES), fixed),
            pl.BlockSpec((CONV_K, 3 * d), fixed),
            pl.BlockSpec((1, LANES), fixed),
            pl.BlockSpec((1, LANES), fixed),
        ],
        out_specs=[
            pl.BlockSpec((tm, NUM_SEG * d), lambda i: (i, 0)),
            pl.BlockSpec((tm, LANES), lambda i: (i, 0)),
        ],
        out_shape=[
            jax.ShapeDtypeStruct((t, NUM_SEG * d), BF16),
            jax.ShapeDtypeStruct((t, LANES), F32),
        ],
        scratch_shapes=[
            pltpu.VMEM((tm, d), BF16),
            pltpu.VMEM((3, tm + SUBLANES, d), F32),
            pltpu.VMEM((3, SUBLANES, d), F32),
        ],
        compiler_params=pltpu.CompilerParams(
            dimension_semantics=("arbitrary",), vmem_limit_bytes=VMEM_LIMIT),
        name="inproj",
    )(x2, n1, w_main, w_ab, conv_w, qw, kw)


def _split3(x):
    h = x.astype(BF16)
    r = x - h.astype(F32)
    m = r.astype(BF16)
    l = (r - m.astype(F32)).astype(BF16)
    return h, m, l


def _gdn_kernel(q_ref, k_ref, v_ref, z_ref, ga_ref, ab_ref, hv_ref, nw_ref, o_ref, state,
                *, rows, chunk, heads):
    s = pl.program_id(1)

    @pl.when(s == 0)
    def _():
        state[...] = jnp.zeros_like(state)

    ri = lax.broadcasted_iota(jnp.int32, (chunk, chunk), 0)
    ci = lax.broadcasted_iota(jnp.int32, (chunk, chunk), 1)
    lower = ri >= ci
    strict = ri > ci
    eye = jnp.where(ri == ci, 1.0, 0.0).astype(F32)
    tri_ones = jnp.where(lower, 1.0, 0.0).astype(BF16)
    n_doublings = int(math.log2(chunk)) - 1
    a_neg = -jnp.exp(hv_ref[0:1, :])
    dt_bias = hv_ref[1:2, :]
    norm_w = nw_ref[...]

    def chunk_body(c, carry_unused):
        r0 = pl.multiple_of(c * chunk, chunk)
        ab = ab_ref[pl.ds(r0, chunk), :]
        xa = ab + dt_bias
        softplus = jnp.maximum(xa, 0.0) + jnp.log1p(jnp.exp(-jnp.abs(xa)))
        g = a_neg * softplus
        beta = _sigmoid(ab)
        g1, g2, g3 = _split3(g)
        gc = (jnp.dot(tri_ones, g1, preferred_element_type=F32)
              + jnp.dot(tri_ones, g2, preferred_element_type=F32)
              + jnp.dot(tri_ones, g3, preferred_element_type=F32))
        gct = gc.T

        hs = range(heads)
        cols = [h * GDN_HEAD_DIM for h in hs]
        q = [q_ref[pl.ds(r0, chunk), c0:c0 + GDN_HEAD_DIM].astype(F32) for c0 in cols]
        k = [k_ref[pl.ds(r0, chunk), c0:c0 + GDN_HEAD_DIM].astype(F32) for c0 in cols]
        v = [v_ref[pl.ds(r0, chunk), c0:c0 + GDN_HEAD_DIM].astype(F32) for c0 in cols]
        gcol = [jnp.broadcast_to(gc[:, h:h + 1], (chunk, GDN_HEAD_DIM)) for h in hs]
        bcol = [jnp.broadcast_to(beta[:, heads + h:heads + h + 1], (chunk, GDN_HEAD_DIM))
                for h in hs]
        glast = [jnp.broadcast_to(gc[chunk - 1:chunk, h:h + 1], (chunk, GDN_HEAD_DIM))
                 for h in hs]
        decay = [jnp.exp(jnp.where(lower, gcol[h] - jnp.broadcast_to(gct[h:h + 1, :],
                                                                     (chunk, chunk)), 0.0))
                 for h in hs]
        gram = [_bdot_nt(jnp.concatenate([k[h], q[h]], axis=0), k[h]) for h in hs]
        a_qk = [jnp.where(lower, gram[h][chunk:] * decay[h], 0.0) for h in hs]
        p = [jnp.where(strict, -(bcol[h] * gram[h][:chunk] * decay[h]), 0.0) for h in hs]
        t_inv = [eye + p[h] for h in hs]
        for _ in range(n_doublings):
            p = [_bdot(p[h], p[h]) for h in hs]
            t_inv = [t_inv[h] + _bdot(t_inv[h], p[h]) for h in hs]

        eg = [jnp.exp(gcol[h]) for h in hs]
        uw = [_bdot(t_inv[h], jnp.concatenate([v[h] * bcol[h], k[h] * bcol[h] * eg[h]], axis=1))
              for h in hs]
        st = [state[h] for h in hs]
        ws = [_bdot(jnp.concatenate([uw[h][:, GDN_HEAD_DIM:], q[h] * eg[h]], axis=0), st[h])
              for h in hs]
        v_new = [uw[h][:, :GDN_HEAD_DIM] - ws[h][:chunk] for h in hs]
        o = [ws[h][chunk:] + _bdot(a_qk[h], v_new[h]) for h in hs]
        for h in hs:
            k_dec = k[h] * jnp.exp(glast[h] - gcol[h])
            state[h] = st[h] * jnp.exp(glast[h]) + _bdot_tn(k_dec, v_new[h])
        for h in hs:
            c0 = cols[h]
            on = o[h] * lax.rsqrt(jnp.mean(o[h] * o[h], axis=-1, keepdims=True) + EPS) * norm_w
            z = z_ref[pl.ds(r0, chunk), c0:c0 + GDN_HEAD_DIM].astype(F32)
            gate = ga_ref[pl.ds(r0, chunk), c0:c0 + GDN_HEAD_DIM].astype(F32)
            o_ref[pl.ds(r0, chunk), c0:c0 + GDN_HEAD_DIM] = (on * z * gate).astype(BF16)
        return carry_unused

    lax.fori_loop(0, rows // chunk, chunk_body, 0)


def _gdn(proj, ab, head_vecs, norm_w, batch, seq, heads):
    t = proj.shape[0]
    width = heads * GDN_HEAD_DIM
    rows = GDN_ROWS
    nsb = seq // rows

    def seg_spec(seg):
        return pl.BlockSpec((rows, width), lambda b, s, seg=seg: (b * nsb + s, seg))

    kern = functools.partial(_gdn_kernel, rows=rows, chunk=GDN_CHUNK, heads=heads)
    return pl.pallas_call(
        kern,
        grid=(batch, nsb),
        in_specs=[
            seg_spec(SEG_GQ), seg_spec(SEG_GK), seg_spec(SEG_GV), seg_spec(SEG_GZ),
            seg_spec(SEG_GATE_A),
            pl.BlockSpec((rows, LANES), lambda b, s: (b * nsb + s, 0)),
            pl.BlockSpec((SUBLANES, LANES), lambda b, s: (0, 0)),
            pl.BlockSpec((1, GDN_HEAD_DIM), lambda b, s: (0, 0)),
        ],
        out_specs=pl.BlockSpec((rows, width), lambda b, s: (b * nsb + s, 0)),
        out_shape=jax.ShapeDtypeStruct((t, width), BF16),
        scratch_shapes=[pltpu.VMEM((heads, GDN_HEAD_DIM, GDN_HEAD_DIM), F32)],
        compiler_params=pltpu.CompilerParams(
            dimension_semantics=("parallel", "arbitrary"), vmem_limit_bytes=VMEM_LIMIT),
        name="gdn",
    )(proj, proj, proj, proj, proj, ab, head_vecs, norm_w)


def _attn_kernel(q_ref, k_ref, v_ref, gate_ref, lam_ref, sw_ref, o_ref, qs_ref, vt_ref,
                 *, blk, seq):
    hd = 2 * DIFF_HEAD_DIM
    nblk = seq // blk
    lane = lax.broadcasted_iota(jnp.int32, (blk, hd), 1)
    lo = lane < DIFF_HEAD_DIM

    for i in range(nblk):
        rows = slice(i * blk, (i + 1) * blk)
        qn = q_ref[rows, :].astype(F32)
        qs_ref[0, rows, :] = jnp.where(lo, qn, 0.0).astype(BF16)
        qs_ref[1, rows, :] = jnp.where(lo, 0.0, qn).astype(BF16)
    for c in range(seq // hd):
        rows = slice(c * hd, (c + 1) * hd)
        vt_ref[:, rows] = v_ref[rows, :].astype(F32).T.astype(BF16)

    lp = lam_ref[...]
    lam = (jnp.exp(jnp.sum(lp[0:1, :] * lp[1:2, :], axis=-1, keepdims=True))
           - jnp.exp(jnp.sum(lp[2:3, :] * lp[3:4, :], axis=-1, keepdims=True)) + LAMBDA_INIT)
    kv_idx = lax.broadcasted_iota(jnp.int32, (blk, 2 * blk), 0)
    q_idx = lax.broadcasted_iota(jnp.int32, (blk, 2 * blk), 1)
    q_idx = jnp.where(q_idx >= blk, q_idx - blk, q_idx)
    causal = kv_idx <= q_idx

    def scores(i, j):
        qs = jnp.concatenate([qs_ref[0, i * blk:(i + 1) * blk, :],
                              qs_ref[1, i * blk:(i + 1) * blk, :]], axis=0)
        return lax.dot_general(k_ref[j * blk:(j + 1) * blk, :], qs, (((1,), (1,)), ((), ())),
                               preferred_element_type=F32)

    pairs = [(i, j) for i in range(nblk) for j in range(i + 1)]
    sc_next = scores(*pairs[0])
    m = l = acc = None
    for n, (i, j) in enumerate(pairs):
        sc = sc_next
        if n + 1 < len(pairs):
            sc_next = scores(*pairs[n + 1])
        if j == i:
            sc = jnp.where(causal, sc, -jnp.inf)
        vt = vt_ref[:, j * blk:(j + 1) * blk]
        if j == 0:
            m = jnp.max(sc, axis=0, keepdims=True)
            p = jnp.exp2(sc - m)
            l = jnp.sum(p, axis=0, keepdims=True)
            acc = jnp.dot(vt, p.astype(BF16), preferred_element_type=F32)
        else:
            m_new = jnp.maximum(m, jnp.max(sc, axis=0, keepdims=True))
            alpha = jnp.exp2(m - m_new)
            p = jnp.exp2(sc - m_new)
            l = alpha * l + jnp.sum(p, axis=0, keepdims=True)
            acc = alpha * acc + jnp.dot(vt, p.astype(BF16), preferred_element_type=F32)
            m = m_new
        if j == i:
            rows = slice(i * blk, (i + 1) * blk)
            on = acc / l
            o = (on[:, :blk] - lam * on[:, blk:]).T
            o = o * lax.rsqrt(jnp.mean(o * o, axis=-1, keepdims=True) + EPS) * sw_ref[...]
            o = o * (1.0 - LAMBDA_INIT)
            o_ref[rows, :] = (o * gate_ref[rows, :].astype(F32)).astype(BF16)


def _attn(proj, lam, sw, batch, seq, heads):
    t = proj.shape[0]
    blk = ATTN_BLOCK
    hd = 2 * DIFF_HEAD_DIM
    kern = functools.partial(_attn_kernel, blk=blk, seq=seq)
    fixed = lambda b, h: (0, 0)
    return pl.pallas_call(
        kern,
        grid=(batch, heads),
        in_specs=[
            pl.BlockSpec((seq, hd), lambda b, h: (b, SEG_DQ * heads + h)),
            pl.BlockSpec((seq, hd), lambda b, h: (b, SEG_DK * heads + h)),
            pl.BlockSpec((seq, hd), lambda b, h: (b, SEG_DV * heads + h)),
            pl.BlockSpec((seq, hd), lambda b, h: (b, SEG_GATE_B * heads + h)),
            pl.BlockSpec((SUBLANES, hd), fixed),
            pl.BlockSpec((1, hd), fixed),
        ],
        out_specs=pl.BlockSpec((seq, hd), lambda b, h: (b, h)),
        out_shape=jax.ShapeDtypeStruct((t, heads * hd), BF16),
        scratch_shapes=[pltpu.VMEM((2, seq, hd), BF16), pltpu.VMEM((hd, seq), BF16)],
        compiler_params=pltpu.CompilerParams(
            dimension_semantics=("parallel", "parallel"), vmem_limit_bytes=VMEM_LIMIT),
        name="diffattn",
    )(proj, proj, proj, proj, lam, sw)


def _mlp_kernel(oa_ref, ob_ref, x_ref, wo_ref, n2_ref, wg_ref, wu_ref, wd_ref, o_ref, u_ref,
                *, tf):
    dff = wg_ref.shape[1]
    mixed = (oa_ref[...].astype(F32) + ob_ref[...].astype(F32)).astype(BF16)
    h = x_ref[...] + jnp.dot(mixed, wo_ref[...], preferred_element_type=F32)
    ms = jnp.mean(h * h, axis=-1, keepdims=True)
    u_ref[...] = (h * lax.rsqrt(ms + EPS) * n2_ref[...]).astype(BF16)

    def gate_up(f):
        u = u_ref[...]
        return (jnp.dot(u, wg_ref[:, f:f + tf], preferred_element_type=F32),
                jnp.dot(u, wu_ref[:, f:f + tf], preferred_element_type=F32))

    acc = h
    starts = list(range(0, dff, tf))
    nxt = gate_up(starts[0])
    for n, f in enumerate(starts):
        g, up = nxt
        if n + 1 < len(starts):
            nxt = gate_up(starts[n + 1])
        act = (g * _sigmoid(g) * up).astype(BF16)
        acc = acc + jnp.dot(act, wd_ref[f:f + tf, :], preferred_element_type=F32)
    o_ref[...] = acc


def _mlp(oa, ob, x2, w_out, n2, wg, wu, wd, tm, tf):
    t, d = x2.shape
    dff = wg.shape[1]
    row = lambda i: (i, 0)
    fixed = lambda i: (0, 0)
    resident = pl.Buffered(1)
    return pl.pallas_call(
        functools.partial(_mlp_kernel, tf=tf),
        grid=(t // tm,),
        in_specs=[
            pl.BlockSpec((tm, d), row), pl.BlockSpec((tm, d), row), pl.BlockSpec((tm, d), row),
            pl.BlockSpec((d, d), fixed, pipeline_mode=resident),
            pl.BlockSpec((1, d), fixed),
            pl.BlockSpec((d, dff), fixed, pipeline_mode=resident),
            pl.BlockSpec((d, dff), fixed, pipeline_mode=resident),
            pl.BlockSpec((dff, d), fixed, pipeline_mode=resident),
        ],
        out_specs=pl.BlockSpec((tm, d), row),
        out_shape=jax.ShapeDtypeStruct((t, d), F32),
        scratch_shapes=[pltpu.VMEM((tm, d), BF16)],
        compiler_params=pltpu.CompilerParams(
            dimension_semantics=("parallel",), vmem_limit_bytes=VMEM_LIMIT),
        name="mlp",
    )(oa, ob, x2, w_out, n2, wg, wu, wd)


def _layer(h2, batch, seq, norm1_w, w_in, conv_w, a_log, dt_bias, gdn_norm_w, q_norm_w, k_norm_w,
           lambda_q1, lambda_k1, lambda_q2, lambda_k2, subln_w, w_out, norm2_w, w_gate, w_up,
           w_down):
    d = h2.shape[1]
    heads = d // GDN_HEAD_DIM
    c0 = 4 * d
    c1 = c0 + 2 * heads
    w_main = jnp.concatenate([w_in[:, :c0], w_in[:, c1:]], axis=1).astype(BF16)
    w_ab = jnp.pad(w_in[:, c0:c1], ((0, 0), (0, LANES - 2 * heads))).astype(BF16)
    head_vecs = jnp.zeros((SUBLANES, LANES), F32)
    head_vecs = head_vecs.at[0, :heads].set(a_log).at[1, :heads].set(dt_bias)
    lam = jnp.zeros((SUBLANES, 2 * DIFF_HEAD_DIM), F32)
    lam = (lam.at[0, :DIFF_HEAD_DIM].set(lambda_q1).at[1, :DIFF_HEAD_DIM].set(lambda_k1)
           .at[2, :DIFF_HEAD_DIM].set(lambda_q2).at[3, :DIFF_HEAD_DIM].set(lambda_k2))
    qw = jnp.tile(q_norm_w, 2)[None, :]
    kw = jnp.tile(k_norm_w, 2)[None, :]

    proj, ab = _inproj(h2, norm1_w[None, :], w_main, w_ab, conv_w, qw, kw, tm=INPROJ_ROWS, seq=seq)
    oa = _gdn(proj, ab, head_vecs, gdn_norm_w[None, :], batch, seq, heads)
    ob = _attn(proj, lam, subln_w[None, :], batch, seq, heads)
    return _mlp(oa, ob, h2, w_out.astype(BF16), norm2_w[None, :], w_gate.astype(BF16),
                w_up.astype(BF16), w_down.astype(BF16), tm=MLP_ROWS, tf=MLP_COLS)


def kernel(x, norm1_w, w_in, conv_w, a_log, dt_bias, gdn_norm_w, q_norm_w, k_norm_w, lambda_q1,
           lambda_k1, lambda_q2, lambda_k2, subln_w, w_out, norm2_w, w_gate, w_up, w_down):
    batch, seq, d = x.shape
    depth = norm1_w.shape[0]
    assert depth == 1, "lambda_init is specialised to a single layer"
    h2 = x.reshape(batch * seq, d)
    for l in range(depth):
        h2 = _layer(h2, batch, seq, norm1_w[l], w_in[l], conv_w[l], a_log[l], dt_bias[l],
                    gdn_norm_w[l], q_norm_w[l], k_norm_w[l], lambda_q1[l], lambda_k1[l],
                    lambda_q2[l], lambda_k2[l], subln_w[l], w_out[l], norm2_w[l], w_gate[l],
                    w_up[l], w_down[l])
    return h2.reshape(batch, seq, d)
```

```python
import functools
import math

import jax
import jax.numpy as jnp
from jax import lax
from jax.experimental import pallas as pl
from jax.experimental.pallas import tpu as pltpu

F32 = jnp.float32
BF16 = jnp.bfloat16

EPS = 1e-6
CONV_K = 4
GDN_HEAD_DIM = 128
DIFF_HEAD_DIM = 64
LAMBDA_INIT = 0.8 - 0.6 * math.exp(-0.3 * 0)

LANES = 128
SUBLANES = 8
VMEM_LIMIT = 56 * 1024 * 1024

SEG_GQ, SEG_GK, SEG_GV, SEG_GZ, SEG_DQ, SEG_DK, SEG_DV, SEG_GATE_A, SEG_GATE_B = range(9)
NUM_SEG = 9

GDN_CHUNK = 128
GDN_ROWS = 1024
INPROJ_ROWS = 256
INPROJ_COLS = 256
ATTN_BLOCK = 256
ATTN_LOOKAHEAD = 4
MLP_ROWS = 256
MLP_COLS = 256


def _bdot(a, b):
    return jnp.dot(a.astype(BF16), b.astype(BF16), preferred_element_type=F32)


def _bdot_nt(a, b):
    return lax.dot_general(a.astype(BF16), b.astype(BF16), (((1,), (1,)), ((), ())),
                           preferred_element_type=F32)


def _bdot_tn(a, b):
    return lax.dot_general(a.astype(BF16), b.astype(BF16), (((0,), (0,)), ((), ())),
                           preferred_element_type=F32)


def _sigmoid(x):
    return 1.0 / (1.0 + jnp.exp2(x * (-math.log2(math.e))))


def _half_rms(x, w, lo):
    x2 = x * x
    s_lo = jnp.sum(jnp.where(lo, x2, 0.0), axis=-1, keepdims=True)
    s_hi = jnp.sum(jnp.where(lo, 0.0, x2), axis=-1, keepdims=True)
    ms = jnp.where(lo, s_lo, s_hi) * (1.0 / DIFF_HEAD_DIM)
    return x * lax.rsqrt(ms + EPS) * w


def _inproj_kernel(x_ref, n1_ref, w_ref, wab_ref, cw_ref, qw_ref, kw_ref, proj_ref, ab_ref,
                   ybuf, halo, *, tm, d, tiles_per_seq):
    i = pl.program_id(0)
    pad = SUBLANES
    x = x_ref[...]
    ms = jnp.mean(x * x, axis=-1, keepdims=True)
    xn = (x * lax.rsqrt(ms + EPS) * n1_ref[...]).astype(BF16)
    ab_ref[...] = jnp.dot(xn, wab_ref[...], preferred_element_type=F32)
    seq_start = (i % tiles_per_seq) == 0
    lane = lax.broadcasted_iota(jnp.int32, (tm, LANES), 1)
    lo = lane < DIFF_HEAD_DIM
    q_scale = (DIFF_HEAD_DIM ** -0.5) * math.log2(math.e)

    def matmul(item):
        seg, cq = item
        col = seg * d + cq * INPROJ_COLS
        return jnp.dot(xn, w_ref[:, col:col + INPROJ_COLS], preferred_element_type=F32)

    def epilogue(item, y):
        seg, cq = item
        lc = cq * INPROJ_COLS
        col = seg * d + lc
        if seg in (SEG_GQ, SEG_GK, SEG_GV):
            hist = jnp.where(seq_start, 0.0, halo[seg, :, lc:lc + INPROJ_COLS])
            ybuf[seg, 0:pad, lc:lc + INPROJ_COLS] = hist
            ybuf[seg, pad:pad + tm, lc:lc + INPROJ_COLS] = y
            halo[seg, :, lc:lc + INPROJ_COLS] = y[tm - pad:tm, :]
            for c0 in range(0, INPROJ_COLS, GDN_HEAD_DIM):
                for r0 in range(0, tm, GDN_CHUNK):
                    win = ybuf[seg, r0:r0 + pad + GDN_CHUNK, lc + c0:lc + c0 + GDN_HEAD_DIM]
                    acc = None
                    for j in range(CONV_K):
                        shift = CONV_K - 1 - j
                        tap = pltpu.roll(win, shift, axis=0) if shift else win
                        term = tap[pad:, :] * cw_ref[j:j + 1, col + c0:col + c0 + GDN_HEAD_DIM]
                        acc = term if acc is None else acc + term
                    v = acc * _sigmoid(acc)
                    if seg != SEG_GV:
                        v = v * lax.rsqrt(jnp.sum(v * v, axis=-1, keepdims=True) + EPS)
                    if seg == SEG_GQ:
                        v = v * (GDN_HEAD_DIM ** -0.5)
                    proj_ref[r0:r0 + GDN_CHUNK, col + c0:col + c0 + GDN_HEAD_DIM] = v.astype(BF16)
        elif seg == SEG_GZ:
            proj_ref[:, col:col + INPROJ_COLS] = (y * _sigmoid(y)).astype(BF16)
        elif seg in (SEG_DQ, SEG_DK):
            w_ref_n = qw_ref if seg == SEG_DQ else kw_ref
            for c0 in range(0, INPROJ_COLS, LANES):
                yn = _half_rms(y[:, c0:c0 + LANES], w_ref_n[...], lo)
                if seg == SEG_DQ:
                    yn = yn * q_scale
                proj_ref[:, col + c0:col + c0 + LANES] = yn.astype(BF16)
        elif seg == SEG_DV:
            proj_ref[:, col:col + INPROJ_COLS] = y.astype(BF16)
        else:
            proj_ref[:, col:col + INPROJ_COLS] = _sigmoid(y).astype(BF16)

    order = (SEG_GQ, SEG_DV, SEG_GK, SEG_GATE_A, SEG_GV, SEG_GATE_B, SEG_DQ, SEG_GZ, SEG_DK)
    items = [(seg, cq) for cq in range(d // INPROJ_COLS) for seg in order]
    y_next = matmul(items[0])
    for n, item in enumerate(items):
        y = y_next
        if n + 1 < len(items):
            y_next = matmul(items[n + 1])
        epilogue(item, y)


def _inproj(x2, n1, w_main, w_ab, conv_w, qw, kw, tm, seq):
    t, d = x2.shape
    fixed = lambda i: (0, 0)
    resident = pl.Buffered(1)
    kern = functools.partial(_inproj_kernel, tm=tm, d=d, tiles_per_seq=seq // tm)
    return pl.pallas_call(
        kern,
        grid=(t // tm,),
        in_specs=[
            pl.BlockSpec((tm, d), lambda i: (i, 0)),
            pl.BlockSpec((1, d), fixed),
            pl.BlockSpec((d, NUM_SEG * d), fixed, pipeline_mode=resident),
            pl.BlockSpec((d, LANES), fixed),
            pl.BlockSpec((CONV_K, 3 * d), fixed),
            pl.BlockSpec((1, LANES), fixed),
            pl.BlockSpec((1, LANES), fixed),
        ],
        out_specs=[
            pl.BlockSpec((tm, NUM_SEG * d), lambda i: (i, 0)),
            pl.BlockSpec((tm, LANES), lambda i: (i, 0)),
        ],
        out_shape=[
            jax.ShapeDtypeStruct((t, NUM_SEG * d), BF16),
            jax.ShapeDtypeStruct((t, LANES), F32),
        ],
        scratch_shapes=[
            pltpu.VMEM((3, tm + SUBLANES, d), F32),
            pltpu.VMEM((3, SUBLANES, d), F32),
        ],
        compiler_params=pltpu.CompilerParams(
            dimension_semantics=("arbitrary",), vmem_limit_bytes=VMEM_LIMIT),
        name="inproj",
    )(x2, n1, w_main, w_ab, conv_w, qw, kw)


def _split3(x):
    h = x.astype(BF16)
    r = x - h.astype(F32)
    m = r.astype(BF16)
    l = (r - m.astype(F32)).astype(BF16)
    return h, m, l


def _gdn_kernel(q_ref, k_ref, v_ref, z_ref, ga_ref, ab_ref, hv_ref, nw_ref, o_ref, state,
                *, rows, chunk, heads):
    s = pl.program_id(1)

    @pl.when(s == 0)
    def _():
        state[...] = jnp.zeros_like(state)

    ri = lax.broadcasted_iota(jnp.int32, (chunk, chunk), 0)
    ci = lax.broadcasted_iota(jnp.int32, (chunk, chunk), 1)
    lower = ri >= ci
    strict = ri > ci
    eye = jnp.where(ri == ci, 1.0, 0.0).astype(F32)
    tri_ones = jnp.where(lower, 1.0, 0.0).astype(BF16)
    n_doublings = int(math.log2(chunk)) - 1
    a_neg = -jnp.exp(hv_ref[0:1, :])
    dt_bias = hv_ref[1:2, :]
    norm_w = nw_ref[...]

    def chunk_body(c, carry_unused):
        r0 = pl.multiple_of(c * chunk, chunk)
        ab = ab_ref[pl.ds(r0, chunk), :]
        xa = ab + dt_bias
        softplus = jnp.maximum(xa, 0.0) + jnp.log1p(jnp.exp(-jnp.abs(xa)))
        g = a_neg * softplus
        beta = _sigmoid(ab)
        g1, g2, g3 = _split3(g)
        gc = (jnp.dot(tri_ones, g1, preferred_element_type=F32)
              + jnp.dot(tri_ones, g2, preferred_element_type=F32)
              + jnp.dot(tri_ones, g3, preferred_element_type=F32))
        gct = gc.T

        hs = range(heads)
        cols = [h * GDN_HEAD_DIM for h in hs]
        q = [q_ref[pl.ds(r0, chunk), c0:c0 + GDN_HEAD_DIM].astype(F32) for c0 in cols]
        k = [k_ref[pl.ds(r0, chunk), c0:c0 + GDN_HEAD_DIM].astype(F32) for c0 in cols]
        v = [v_ref[pl.ds(r0, chunk), c0:c0 + GDN_HEAD_DIM].astype(F32) for c0 in cols]
        gcol = [jnp.broadcast_to(gc[:, h:h + 1], (chunk, GDN_HEAD_DIM)) for h in hs]
        bcol = [jnp.broadcast_to(beta[:, heads + h:heads + h + 1], (chunk, GDN_HEAD_DIM))
                for h in hs]
        glast = [jnp.broadcast_to(gc[chunk - 1:chunk, h:h + 1], (chunk, GDN_HEAD_DIM))
                 for h in hs]
        decay = [jnp.exp(jnp.where(lower, gcol[h] - jnp.broadcast_to(gct[h:h + 1, :],
                                                                     (chunk, chunk)), 0.0))
                 for h in hs]
        gram = [_bdot_nt(jnp.concatenate([k[h], q[h]], axis=0), k[h]) for h in hs]
        a_qk = [jnp.where(lower, gram[h][chunk:] * decay[h], 0.0) for h in hs]
        p = [jnp.where(strict, -(bcol[h] * gram[h][:chunk] * decay[h]), 0.0) for h in hs]
        t_inv = [eye + p[h] for h in hs]
        for _ in range(n_doublings):
            p = [_bdot(p[h], p[h]) for h in hs]
            t_inv = [t_inv[h] + _bdot(t_inv[h], p[h]) for h in hs]

        eg = [jnp.exp(gcol[h]) for h in hs]
        uw = [_bdot(t_inv[h], jnp.concatenate([v[h] * bcol[h], k[h] * bcol[h] * eg[h]], axis=1))
              for h in hs]
        st = [state[h] for h in hs]
        ws = [_bdot(jnp.concatenate([uw[h][:, GDN_HEAD_DIM:], q[h] * eg[h]], axis=0), st[h])
              for h in hs]
        v_new = [uw[h][:, :GDN_HEAD_DIM] - ws[h][:chunk] for h in hs]
        o = [ws[h][chunk:] + _bdot(a_qk[h], v_new[h]) for h in hs]
        for h in hs:
            k_dec = k[h] * jnp.exp(glast[h] - gcol[h])
            state[h] = st[h] * jnp.exp(glast[h]) + _bdot_tn(k_dec, v_new[h])
        for h in hs:
            c0 = cols[h]
            on = o[h] * lax.rsqrt(jnp.mean(o[h] * o[h], axis=-1, keepdims=True) + EPS) * norm_w
            z = z_ref[pl.ds(r0, chunk), c0:c0 + GDN_HEAD_DIM].astype(F32)
            gate = ga_ref[pl.ds(r0, chunk), c0:c0 + GDN_HEAD_DIM].astype(F32)
            o_ref[pl.ds(r0, chunk), c0:c0 + GDN_HEAD_DIM] = (on * z * gate).astype(BF16)
        return carry_unused

    lax.fori_loop(0, rows // chunk, chunk_body, 0)


def _gdn(proj, ab, head_vecs, norm_w, batch, seq, heads):
    t = proj.shape[0]
    width = heads * GDN_HEAD_DIM
    rows = GDN_ROWS
    nsb = seq // rows

    def seg_spec(seg):
        return pl.BlockSpec((rows, width), lambda b, s, seg=seg: (b * nsb + s, seg))

    kern = functools.partial(_gdn_kernel, rows=rows, chunk=GDN_CHUNK, heads=heads)
    return pl.pallas_call(
        kern,
        grid=(batch, nsb),
        in_specs=[
            seg_spec(SEG_GQ), seg_spec(SEG_GK), seg_spec(SEG_GV), seg_spec(SEG_GZ),
            seg_spec(SEG_GATE_A),
            pl.BlockSpec((rows, LANES), lambda b, s: (b * nsb + s, 0)),
            pl.BlockSpec((SUBLANES, LANES), lambda b, s: (0, 0)),
            pl.BlockSpec((1, GDN_HEAD_DIM), lambda b, s: (0, 0)),
        ],
        out_specs=pl.BlockSpec((rows, width), lambda b, s: (b * nsb + s, 0)),
        out_shape=jax.ShapeDtypeStruct((t, width), BF16),
        scratch_shapes=[pltpu.VMEM((heads, GDN_HEAD_DIM, GDN_HEAD_DIM), F32)],
        compiler_params=pltpu.CompilerParams(
            dimension_semantics=("parallel", "arbitrary"), vmem_limit_bytes=VMEM_LIMIT),
        name="gdn",
    )(proj, proj, proj, proj, proj, ab, head_vecs, norm_w)


def _attn_kernel(q_ref, k_ref, v_ref, gate_ref, lam_ref, sw_ref, o_ref, qs_ref, vt_ref,
                 *, blk, seq):
    hd = 2 * DIFF_HEAD_DIM
    nblk = seq // blk
    lane = lax.broadcasted_iota(jnp.int32, (blk, hd), 1)
    lo = lane < DIFF_HEAD_DIM

    for i in range(nblk):
        rows = slice(i * blk, (i + 1) * blk)
        qn = q_ref[rows, :].astype(F32)
        qs_ref[0, rows, :] = jnp.where(lo, qn, 0.0).astype(BF16)
        qs_ref[1, rows, :] = jnp.where(lo, 0.0, qn).astype(BF16)
    for c in range(seq // hd):
        rows = slice(c * hd, (c + 1) * hd)
        vt_ref[:, rows] = v_ref[rows, :].astype(F32).T.astype(BF16)

    lp = lam_ref[...]
    lam = (jnp.exp(jnp.sum(lp[0:1, :] * lp[1:2, :], axis=-1, keepdims=True))
           - jnp.exp(jnp.sum(lp[2:3, :] * lp[3:4, :], axis=-1, keepdims=True)) + LAMBDA_INIT)
    kv_idx = lax.broadcasted_iota(jnp.int32, (blk, blk), 0)
    q_idx = lax.broadcasted_iota(jnp.int32, (blk, blk), 1)
    causal = kv_idx <= q_idx

    def scores(item):
        i, j, c = item
        return lax.dot_general(k_ref[j * blk:(j + 1) * blk, :],
                               qs_ref[c, i * blk:(i + 1) * blk, :], (((1,), (1,)), ((), ())),
                               preferred_element_type=F32)

    items = [(i, j, c) for i in range(nblk) for j in range(i + 1) for c in range(2)]
    ahead = [scores(items[n]) for n in range(ATTN_LOOKAHEAD)]
    stats = [None, None]
    for n, (i, j, c) in enumerate(items):
        sc = ahead.pop(0)
        if n + ATTN_LOOKAHEAD < len(items):
            ahead.append(scores(items[n + ATTN_LOOKAHEAD]))
        if j == i:
            sc = jnp.where(causal, sc, -jnp.inf)
        vt = vt_ref[:, j * blk:(j + 1) * blk]
        if j == 0:
            m = jnp.max(sc, axis=0, keepdims=True)
            p = jnp.exp2(sc - m)
            l = jnp.sum(p, axis=0, keepdims=True)
            acc = jnp.dot(vt, p.astype(BF16), preferred_element_type=F32)
        else:
            m, l, acc = stats[c]
            m_new = jnp.maximum(m, jnp.max(sc, axis=0, keepdims=True))
            alpha = jnp.exp2(m - m_new)
            p = jnp.exp2(sc - m_new)
            l = alpha * l + jnp.sum(p, axis=0, keepdims=True)
            acc = alpha * acc + jnp.dot(vt, p.astype(BF16), preferred_element_type=F32)
            m = m_new
        stats[c] = (m, l, acc)
        if j == i and c == 1:
            rows = slice(i * blk, (i + 1) * blk)
            on1 = stats[0][2] / stats[0][1]
            on2 = stats[1][2] / stats[1][1]
            o = (on1 - lam * on2).T
            o = o * lax.rsqrt(jnp.mean(o * o, axis=-1, keepdims=True) + EPS) * sw_ref[...]
            o = o * (1.0 - LAMBDA_INIT)
            o_ref[rows, :] = (o * gate_ref[rows, :].astype(F32)).astype(BF16)


def _attn(proj, lam, sw, batch, seq, heads):
    t = proj.shape[0]
    blk = ATTN_BLOCK
    hd = 2 * DIFF_HEAD_DIM
    kern = functools.partial(_attn_kernel, blk=blk, seq=seq)
    fixed = lambda b, h: (0, 0)
    return pl.pallas_call(
        kern,
        grid=(batch, heads),
        in_specs=[
            pl.BlockSpec((seq, hd), lambda b, h: (b, SEG_DQ * heads + h)),
            pl.BlockSpec((seq, hd), lambda b, h: (b, SEG_DK * heads + h)),
            pl.BlockSpec((seq, hd), lambda b, h: (b, SEG_DV * heads + h)),
            pl.BlockSpec((seq, hd), lambda b, h: (b, SEG_GATE_B * heads + h)),
            pl.BlockSpec((SUBLANES, hd), fixed),
            pl.BlockSpec((1, hd), fixed),
        ],
        out_specs=pl.BlockSpec((seq, hd), lambda b, h: (b, h)),
        out_shape=jax.ShapeDtypeStruct((t, heads * hd), BF16),
        scratch_shapes=[pltpu.VMEM((2, seq, hd), BF16), pltpu.VMEM((hd, seq), BF16)],
        compiler_params=pltpu.CompilerParams(
            dimension_semantics=("parallel", "parallel"), vmem_limit_bytes=VMEM_LIMIT),
        name="diffattn",
    )(proj, proj, proj, proj, lam, sw)


def _mlp_kernel(oa_ref, ob_ref, x_ref, wo_ref, n2_ref, wg_ref, wu_ref, wd_ref, o_ref, u_ref,
                *, tf):
    dff = wg_ref.shape[1]
    mixed = (oa_ref[...].astype(F32) + ob_ref[...].astype(F32)).astype(BF16)
    h = x_ref[...] + jnp.dot(mixed, wo_ref[...], preferred_element_type=F32)
    ms = jnp.mean(h * h, axis=-1, keepdims=True)
    u_ref[...] = (h * lax.rsqrt(ms + EPS) * n2_ref[...]).astype(BF16)

    def gate_up(f):
        u = u_ref[...]
        return (jnp.dot(u, wg_ref[:, f:f + tf], preferred_element_type=F32),
                jnp.dot(u, wu_ref[:, f:f + tf], preferred_element_type=F32))

    acc = h
    starts = list(range(0, dff, tf))
    nxt = gate_up(starts[0])
    for n, f in enumerate(starts):
        g, up = nxt
        if n + 1 < len(starts):
            nxt = gate_up(starts[n + 1])
        act = (g * _sigmoid(g) * up).astype(BF16)
        acc = acc + jnp.dot(act, wd_ref[f:f + tf, :], preferred_element_type=F32)
    o_ref[...] = acc


def _mlp(oa, ob, x2, w_out, n2, wg, wu, wd, tm, tf):
    t, d = x2.shape
    dff = wg.shape[1]
    row = lambda i: (i, 0)
    fixed = lambda i: (0, 0)
    resident = pl.Buffered(1)
    return pl.pallas_call(
        functools.partial(_mlp_kernel, tf=tf),
        grid=(t // tm,),
        in_specs=[
            pl.BlockSpec((tm, d), row), pl.BlockSpec((tm, d), row), pl.BlockSpec((tm, d), row),
            pl.BlockSpec((d, d), fixed, pipeline_mode=resident),
            pl.BlockSpec((1, d), fixed),
            pl.BlockSpec((d, dff), fixed, pipeline_mode=resident),
            pl.BlockSpec((d, dff), fixed, pipeline_mode=resident),
            pl.BlockSpec((dff, d), fixed, pipeline_mode=resident),
        ],
        out_specs=pl.BlockSpec((tm, d), row),
        out_shape=jax.ShapeDtypeStruct((t, d), F32),
        scratch_shapes=[pltpu.VMEM((tm, d), BF16)],
        compiler_params=pltpu.CompilerParams(
            dimension_semantics=("parallel",), vmem_limit_bytes=VMEM_LIMIT),
        name="mlp",
    )(oa, ob, x2, w_out, n2, wg, wu, wd)


def _layer(h2, batch, seq, norm1_w, w_in, conv_w, a_log, dt_bias, gdn_norm_w, q_norm_w, k_norm_w,
           lambda_q1, lambda_k1, lambda_q2, lambda_k2, subln_w, w_out, norm2_w, w_gate, w_up,
           w_down):
    d = h2.shape[1]
    heads = d // GDN_HEAD_DIM
    c0 = 4 * d
    c1 = c0 + 2 * heads
    w_main = jnp.concatenate([w_in[:, :c0], w_in[:, c1:]], axis=1).astype(BF16)
    w_ab = jnp.pad(w_in[:, c0:c1], ((0, 0), (0, LANES - 2 * heads))).astype(BF16)
    head_vecs = jnp.zeros((SUBLANES, LANES), F32)
    head_vecs = head_vecs.at[0, :heads].set(a_log).at[1, :heads].set(dt_bias)
    lam = jnp.zeros((SUBLANES, 2 * DIFF_HEAD_DIM), F32)
    lam = (lam.at[0, :DIFF_HEAD_DIM].set(lambda_q1).at[1, :DIFF_HEAD_DIM].set(lambda_k1)
           .at[2, :DIFF_HEAD_DIM].set(lambda_q2).at[3, :DIFF_HEAD_DIM].set(lambda_k2))
    qw = jnp.tile(q_norm_w, 2)[None, :]
    kw = jnp.tile(k_norm_w, 2)[None, :]

    proj, ab = _inproj(h2, norm1_w[None, :], w_main, w_ab, conv_w, qw, kw, tm=INPROJ_ROWS, seq=seq)
    oa = _gdn(proj, ab, head_vecs, gdn_norm_w[None, :], batch, seq, heads)
    ob = _attn(proj, lam, subln_w[None, :], batch, seq, heads)
    return _mlp(oa, ob, h2, w_out.astype(BF16), norm2_w[None, :], w_gate.astype(BF16),
                w_up.astype(BF16), w_down.astype(BF16), tm=MLP_ROWS, tf=MLP_COLS)


def kernel(x, norm1_w, w_in, conv_w, a_log, dt_bias, gdn_norm_w, q_norm_w, k_norm_w, lambda_q1,
           lambda_k1, lambda_q2, lambda_k2, subln_w, w_out, norm2_w, w_gate, w_up, w_down):
    batch, seq, d = x.shape
    depth = norm1_w.shape[0]
    assert depth == 1, "lambda_init is specialised to a single layer"
    h2 = x.reshape(batch * seq, d)
    for l in range(depth):
        h2 = _layer(h2, batch, seq, norm1_w[l], w_in[l], conv_w[l], a_log[l], dt_bias[l],
                    gdn_norm_w[l], q_norm_w[l], k_norm_w[l], lambda_q1[l], lambda_k1[l],
                    lambda_q2[l], lambda_k2[l], subln_w[l], w_out[l], norm2_w[l], w_gate[l],
                    w_up[l], w_down[l])
    return h2.reshape(batch, seq, d)
```

```python
import functools
import math

import jax
import jax.numpy as jnp
from jax import lax
from jax.experimental import pallas as pl
from jax.experimental.pallas import tpu as pltpu

F32 = jnp.float32
BF16 = jnp.bfloat16

EPS = 1e-6
CONV_K = 4
GDN_HEAD_DIM = 128
DIFF_HEAD_DIM = 64
LAMBDA_INIT = 0.8 - 0.6 * math.exp(-0.3 * 0)

LANES = 128
SUBLANES = 8
VMEM_LIMIT = 56 * 1024 * 1024

SEG_GQ, SEG_GK, SEG_GV, SEG_GZ, SEG_DQ, SEG_DK, SEG_DV, SEG_GATE_A, SEG_GATE_B = range(9)
NUM_SEG = 9

GDN_CHUNK = 128
GDN_ROWS = 1024
GDN_SKEW = 6
INPROJ_ROWS = 256
INPROJ_COLS = 256
ATTN_BLOCK = 256
ATTN_LOOKAHEAD = 4
MLP_ROWS = 256
MLP_COLS = 256


def _bdot(a, b):
    return jnp.dot(a.astype(BF16), b.astype(BF16), preferred_element_type=F32)


def _bdot_nt(a, b):
    return lax.dot_general(a.astype(BF16), b.astype(BF16), (((1,), (1,)), ((), ())),
                           preferred_element_type=F32)


def _bdot_tn(a, b):
    return lax.dot_general(a.astype(BF16), b.astype(BF16), (((0,), (0,)), ((), ())),
                           preferred_element_type=F32)


def _sigmoid(x):
    return 1.0 / (1.0 + jnp.exp2(x * (-math.log2(math.e))))


def _half_rms(x, w, lo):
    x2 = x * x
    s_lo = jnp.sum(jnp.where(lo, x2, 0.0), axis=-1, keepdims=True)
    s_hi = jnp.sum(jnp.where(lo, 0.0, x2), axis=-1, keepdims=True)
    ms = jnp.where(lo, s_lo, s_hi) * (1.0 / DIFF_HEAD_DIM)
    return x * lax.rsqrt(ms + EPS) * w


def _inproj_kernel(x_ref, n1_ref, wlo_ref, whi_ref, wab_ref, cw_ref, qw_ref, kw_ref, proj_ref,
                   ab_ref, ybuf, halo, *, tm, d, tiles_per_seq):
    i = pl.program_id(0)
    pad = SUBLANES
    x = x_ref[...]
    ms = jnp.mean(x * x, axis=-1, keepdims=True)
    xn = (x * lax.rsqrt(ms + EPS) * n1_ref[...]).astype(BF16)
    ab_ref[...] = jnp.dot(xn, wab_ref[...], preferred_element_type=F32)
    seq_start = (i % tiles_per_seq) == 0
    lane = lax.broadcasted_iota(jnp.int32, (tm, LANES), 1)
    lo = lane < DIFF_HEAD_DIM
    q_scale = (DIFF_HEAD_DIM ** -0.5) * math.log2(math.e)

    def matmul(item):
        seg, cq = item
        col = seg * d + cq * INPROJ_COLS
        w_ref, wcol = (wlo_ref, col) if seg < SEG_DQ else (whi_ref, col - SEG_DQ * d)
        return jnp.dot(xn, w_ref[:, wcol:wcol + INPROJ_COLS], preferred_element_type=F32)

    def epilogue(item, y):
        seg, cq = item
        lc = cq * INPROJ_COLS
        col = seg * d + lc
        if seg in (SEG_GQ, SEG_GK, SEG_GV):
            hist = jnp.where(seq_start, 0.0, halo[seg, :, lc:lc + INPROJ_COLS])
            ybuf[seg, 0:pad, lc:lc + INPROJ_COLS] = hist
            ybuf[seg, pad:pad + tm, lc:lc + INPROJ_COLS] = y
            halo[seg, :, lc:lc + INPROJ_COLS] = y[tm - pad:tm, :]
            for c0 in range(0, INPROJ_COLS, GDN_HEAD_DIM):
                for r0 in range(0, tm, GDN_CHUNK):
                    win = ybuf[seg, r0:r0 + pad + GDN_CHUNK, lc + c0:lc + c0 + GDN_HEAD_DIM]
                    acc = None
                    for j in range(CONV_K):
                        shift = CONV_K - 1 - j
                        tap = pltpu.roll(win, shift, axis=0) if shift else win
                        term = tap[pad:, :] * cw_ref[j:j + 1, col + c0:col + c0 + GDN_HEAD_DIM]
                        acc = term if acc is None else acc + term
                    v = acc * _sigmoid(acc)
                    if seg != SEG_GV:
                        v = v * lax.rsqrt(jnp.sum(v * v, axis=-1, keepdims=True) + EPS)
                    if seg == SEG_GQ:
                        v = v * (GDN_HEAD_DIM ** -0.5)
                    proj_ref[r0:r0 + GDN_CHUNK, col + c0:col + c0 + GDN_HEAD_DIM] = v.astype(BF16)
        elif seg == SEG_GZ:
            proj_ref[:, col:col + INPROJ_COLS] = (y * _sigmoid(y)).astype(BF16)
        elif seg in (SEG_DQ, SEG_DK):
            w_ref_n = qw_ref if seg == SEG_DQ else kw_ref
            for c0 in range(0, INPROJ_COLS, LANES):
                yn = _half_rms(y[:, c0:c0 + LANES], w_ref_n[...], lo)
                if seg == SEG_DQ:
                    yn = yn * q_scale
                proj_ref[:, col + c0:col + c0 + LANES] = yn.astype(BF16)
        elif seg == SEG_DV:
            proj_ref[:, col:col + INPROJ_COLS] = y.astype(BF16)
        else:
            proj_ref[:, col:col + INPROJ_COLS] = _sigmoid(y).astype(BF16)

    order = (SEG_GQ, SEG_DV, SEG_GK, SEG_GATE_A, SEG_GV, SEG_GATE_B, SEG_DQ, SEG_GZ, SEG_DK)
    items = [(seg, cq) for cq in range(d // INPROJ_COLS) for seg in order]
    y_next = matmul(items[0])
    for n, item in enumerate(items):
        y = y_next
        if n + 1 < len(items):
            y_next = matmul(items[n + 1])
        epilogue(item, y)


def _inproj(x2, n1, w_lo, w_hi, w_ab, conv_w, qw, kw, tm, seq):
    t, d = x2.shape
    fixed = lambda i: (0, 0)
    resident = pl.Buffered(1)
    kern = functools.partial(_inproj_kernel, tm=tm, d=d, tiles_per_seq=seq // tm)
    return pl.pallas_call(
        kern,
        grid=(t // tm,),
        in_specs=[
            pl.BlockSpec((tm, d), lambda i: (i, 0)),
            pl.BlockSpec((1, d), fixed),
            pl.BlockSpec((d, SEG_DQ * d), fixed, pipeline_mode=resident),
            pl.BlockSpec((d, (NUM_SEG - SEG_DQ) * d), fixed, pipeline_mode=resident),
            pl.BlockSpec((d, LANES), fixed),
            pl.BlockSpec((CONV_K, 3 * d), fixed),
            pl.BlockSpec((1, LANES), fixed),
            pl.BlockSpec((1, LANES), fixed),
        ],
        out_specs=[
            pl.BlockSpec((tm, NUM_SEG * d), lambda i: (i, 0)),
            pl.BlockSpec((tm, LANES), lambda i: (i, 0)),
        ],
        out_shape=[
            jax.ShapeDtypeStruct((t, NUM_SEG * d), BF16),
            jax.ShapeDtypeStruct((t, LANES), F32),
        ],
        scratch_shapes=[
            pltpu.VMEM((3, tm + SUBLANES, d), F32),
            pltpu.VMEM((3, SUBLANES, d), F32),
        ],
        compiler_params=pltpu.CompilerParams(
            dimension_semantics=("arbitrary",), vmem_limit_bytes=VMEM_LIMIT),
        name="inproj",
    )(x2, n1, w_lo, w_hi, w_ab, conv_w, qw, kw)


def _split3(x):
    h = x.astype(BF16)
    r = x - h.astype(F32)
    m = r.astype(BF16)
    l = (r - m.astype(F32)).astype(BF16)
    return h, m, l


def _gdn_kernel(q_ref, k_ref, v_ref, z_ref, ga_ref, ab_ref, hv_ref, nw_ref, o_ref, state,
                *, rows, chunk, heads):
    s = pl.program_id(1)

    @pl.when(s == 0)
    def _():
        state[...] = jnp.zeros_like(state)

    ri = lax.broadcasted_iota(jnp.int32, (chunk, chunk), 0)
    ci = lax.broadcasted_iota(jnp.int32, (chunk, chunk), 1)
    lower = ri >= ci
    strict = ri > ci
    eye = jnp.where(ri == ci, 1.0, 0.0).astype(F32)
    tri_ones = jnp.where(lower, 1.0, 0.0).astype(BF16)
    n_doublings = int(math.log2(chunk)) - 1
    a_neg = -jnp.exp(hv_ref[0:1, :])
    dt_bias = hv_ref[1:2, :]
    norm_w = nw_ref[...]

    hs = range(heads)
    cols = [h * GDN_HEAD_DIM for h in hs]

    def chunk_stages(r0):
        rows_c = slice(r0, r0 + chunk)
        ab = ab_ref[rows_c, :]
        xa = ab + dt_bias
        softplus = jnp.maximum(xa, 0.0) + jnp.log1p(jnp.exp(-jnp.abs(xa)))
        g = a_neg * softplus
        beta = _sigmoid(ab)
        g1, g2, g3 = _split3(g)
        gc = (jnp.dot(tri_ones, g1, preferred_element_type=F32)
              + jnp.dot(tri_ones, g2, preferred_element_type=F32)
              + jnp.dot(tri_ones, g3, preferred_element_type=F32))
        gct = gc.T
        q = [q_ref[rows_c, c0:c0 + GDN_HEAD_DIM].astype(F32) for c0 in cols]
        k = [k_ref[rows_c, c0:c0 + GDN_HEAD_DIM].astype(F32) for c0 in cols]
        v = [v_ref[rows_c, c0:c0 + GDN_HEAD_DIM].astype(F32) for c0 in cols]
        gcol = [jnp.broadcast_to(gc[:, h:h + 1], (chunk, GDN_HEAD_DIM)) for h in hs]
        bcol = [jnp.broadcast_to(beta[:, heads + h:heads + h + 1], (chunk, GDN_HEAD_DIM))
                for h in hs]
        glast = [jnp.broadcast_to(gc[chunk - 1:chunk, h:h + 1], (chunk, GDN_HEAD_DIM))
                 for h in hs]
        decay = [jnp.exp(jnp.where(lower, gcol[h] - jnp.broadcast_to(gct[h:h + 1, :],
                                                                     (chunk, chunk)), 0.0))
                 for h in hs]
        gram = [_bdot_nt(jnp.concatenate([k[h], q[h]], axis=0), k[h]) for h in hs]
        yield
        a_qk = [jnp.where(lower, gram[h][chunk:] * decay[h], 0.0) for h in hs]
        p = [jnp.where(strict, -(bcol[h] * gram[h][:chunk] * decay[h]), 0.0) for h in hs]
        t_inv = [eye + p[h] for h in hs]
        for _ in range(n_doublings):
            p = [_bdot(p[h], p[h]) for h in hs]
            yield
            t_inv = [t_inv[h] + _bdot(t_inv[h], p[h]) for h in hs]
            yield
        eg = [jnp.exp(gcol[h]) for h in hs]
        uw = [_bdot(t_inv[h], jnp.concatenate([v[h] * bcol[h], k[h] * bcol[h] * eg[h]], axis=1))
              for h in hs]
        yield
        st = [state[h] for h in hs]
        ws = [_bdot(jnp.concatenate([uw[h][:, GDN_HEAD_DIM:], q[h] * eg[h]], axis=0), st[h])
              for h in hs]
        yield
        v_new = [uw[h][:, :GDN_HEAD_DIM] - ws[h][:chunk] for h in hs]
        o = [ws[h][chunk:] + _bdot(a_qk[h], v_new[h]) for h in hs]
        for h in hs:
            k_dec = k[h] * jnp.exp(glast[h] - gcol[h])
            state[h] = st[h] * jnp.exp(glast[h]) + _bdot_tn(k_dec, v_new[h])
        yield
        for h in hs:
            c0 = cols[h]
            on = o[h] * lax.rsqrt(jnp.mean(o[h] * o[h], axis=-1, keepdims=True) + EPS) * norm_w
            z = z_ref[rows_c, c0:c0 + GDN_HEAD_DIM].astype(F32)
            gate = ga_ref[rows_c, c0:c0 + GDN_HEAD_DIM].astype(F32)
            o_ref[rows_c, c0:c0 + GDN_HEAD_DIM] = (on * z * gate).astype(BF16)
        yield

    n_stages = 2 * n_doublings + 5
    assert GDN_SKEW >= 2
    gens = [chunk_stages(c * chunk) for c in range(rows // chunk)]
    for step in range(GDN_SKEW * (len(gens) - 1) + n_stages):
        for c, gen in enumerate(gens):
            if 0 <= step - GDN_SKEW * c < n_stages:
                next(gen)


def _gdn(proj, ab, head_vecs, norm_w, batch, seq, heads):
    t = proj.shape[0]
    width = heads * GDN_HEAD_DIM
    rows = GDN_ROWS
    nsb = seq // rows

    def seg_spec(seg):
        return pl.BlockSpec((rows, width), lambda b, s, seg=seg: (b * nsb + s, seg))

    kern = functools.partial(_gdn_kernel, rows=rows, chunk=GDN_CHUNK, heads=heads)
    return pl.pallas_call(
        kern,
        grid=(batch, nsb),
        in_specs=[
            seg_spec(SEG_GQ), seg_spec(SEG_GK), seg_spec(SEG_GV), seg_spec(SEG_GZ),
            seg_spec(SEG_GATE_A),
            pl.BlockSpec((rows, LANES), lambda b, s: (b * nsb + s, 0)),
            pl.BlockSpec((SUBLANES, LANES), lambda b, s: (0, 0)),
            pl.BlockSpec((1, GDN_HEAD_DIM), lambda b, s: (0, 0)),
        ],
        out_specs=pl.BlockSpec((rows, width), lambda b, s: (b * nsb + s, 0)),
        out_shape=jax.ShapeDtypeStruct((t, width), BF16),
        scratch_shapes=[pltpu.VMEM((heads, GDN_HEAD_DIM, GDN_HEAD_DIM), F32)],
        compiler_params=pltpu.CompilerParams(
            dimension_semantics=("parallel", "arbitrary"), vmem_limit_bytes=VMEM_LIMIT),
        name="gdn",
    )(proj, proj, proj, proj, proj, ab, head_vecs, norm_w)


def _attn_kernel(q_ref, k_ref, v_ref, gate_ref, lam_ref, sw_ref, o_ref, qs_ref, vt_ref,
                 *, blk, seq):
    hd = 2 * DIFF_HEAD_DIM
    nblk = seq // blk
    lane = lax.broadcasted_iota(jnp.int32, (blk, hd), 1)
    lo = lane < DIFF_HEAD_DIM

    for i in range(nblk):
        rows = slice(i * blk, (i + 1) * blk)
        qn = q_ref[rows, :].astype(F32)
        qs_ref[0, rows, :] = jnp.where(lo, qn, 0.0).astype(BF16)
        qs_ref[1, rows, :] = jnp.where(lo, 0.0, qn).astype(BF16)
    for c in range(seq // hd):
        rows = slice(c * hd, (c + 1) * hd)
        vt_ref[:, rows] = v_ref[rows, :].astype(F32).T.astype(BF16)

    lp = lam_ref[...]
    lam = (jnp.exp(jnp.sum(lp[0:1, :] * lp[1:2, :], axis=-1, keepdims=True))
           - jnp.exp(jnp.sum(lp[2:3, :] * lp[3:4, :], axis=-1, keepdims=True)) + LAMBDA_INIT)
    kv_idx = lax.broadcasted_iota(jnp.int32, (blk, blk), 0)
    q_idx = lax.broadcasted_iota(jnp.int32, (blk, blk), 1)
    causal = kv_idx <= q_idx

    def scores(item):
        i, j, c = item
        return lax.dot_general(k_ref[j * blk:(j + 1) * blk, :],
                               qs_ref[c, i * blk:(i + 1) * blk, :], (((1,), (1,)), ((), ())),
                               preferred_element_type=F32)

    items = [(i, j, c) for i in range(nblk) for j in range(i + 1) for c in range(2)]
    ahead = [scores(items[n]) for n in range(ATTN_LOOKAHEAD)]
    stats = [None, None]
    for n, (i, j, c) in enumerate(items):
        sc = ahead.pop(0)
        if n + ATTN_LOOKAHEAD < len(items):
            ahead.append(scores(items[n + ATTN_LOOKAHEAD]))
        if j == i:
            sc = jnp.where(causal, sc, -jnp.inf)
        vt = vt_ref[:, j * blk:(j + 1) * blk]
        if j == 0:
            m = jnp.max(sc, axis=0, keepdims=True)
            p = jnp.exp2(sc - m)
            l = jnp.sum(p, axis=0, keepdims=True)
            acc = jnp.dot(vt, p.astype(BF16), preferred_element_type=F32)
        else:
            m, l, acc = stats[c]
            m_new = jnp.maximum(m, jnp.max(sc, axis=0, keepdims=True))
            alpha = jnp.exp2(m - m_new)
            p = jnp.exp2(sc - m_new)
            l = alpha * l + jnp.sum(p, axis=0, keepdims=True)
            acc = alpha * acc + jnp.dot(vt, p.astype(BF16), preferred_element_type=F32)
            m = m_new
        stats[c] = (m, l, acc)
        if j == i and c == 1:
            rows = slice(i * blk, (i + 1) * blk)
            on1 = stats[0][2] / stats[0][1]
            on2 = stats[1][2] / stats[1][1]
            o = (on1 - lam * on2).T
            o = o * lax.rsqrt(jnp.mean(o * o, axis=-1, keepdims=True) + EPS) * sw_ref[...]
            o = o * (1.0 - LAMBDA_INIT)
            o_ref[rows, :] = (o * gate_ref[rows, :].astype(F32)).astype(BF16)


def _attn(proj, lam, sw, batch, seq, heads):
    t = proj.shape[0]
    blk = ATTN_BLOCK
    hd = 2 * DIFF_HEAD_DIM
    kern = functools.partial(_attn_kernel, blk=blk, seq=seq)
    fixed = lambda b, h: (0, 0)
    return pl.pallas_call(
        kern,
        grid=(batch, heads),
        in_specs=[
            pl.BlockSpec((seq, hd), lambda b, h: (b, SEG_DQ * heads + h)),
            pl.BlockSpec((seq, hd), lambda b, h: (b, SEG_DK * heads + h)),
            pl.BlockSpec((seq, hd), lambda b, h: (b, SEG_DV * heads + h)),
            pl.BlockSpec((seq, hd), lambda b, h: (b, SEG_GATE_B * heads + h)),
            pl.BlockSpec((SUBLANES, hd), fixed),
            pl.BlockSpec((1, hd), fixed),
        ],
        out_specs=pl.BlockSpec((seq, hd), lambda b, h: (b, h)),
        out_shape=jax.ShapeDtypeStruct((t, heads * hd), BF16),
        scratch_shapes=[pltpu.VMEM((2, seq, hd), BF16), pltpu.VMEM((hd, seq), BF16)],
        compiler_params=pltpu.CompilerParams(
            dimension_semantics=("parallel", "parallel"), vmem_limit_bytes=VMEM_LIMIT),
        name="diffattn",
    )(proj, proj, proj, proj, lam, sw)


def _mlp_kernel(oa_ref, ob_ref, x_ref, wo_ref, n2_ref, wg_ref, wu_ref, wd_ref, o_ref, u_ref,
                *, tf):
    dff = wg_ref.shape[1]
    mixed = (oa_ref[...].astype(F32) + ob_ref[...].astype(F32)).astype(BF16)
    h = x_ref[...] + jnp.dot(mixed, wo_ref[...], preferred_element_type=F32)
    ms = jnp.mean(h * h, axis=-1, keepdims=True)
    u_ref[...] = (h * lax.rsqrt(ms + EPS) * n2_ref[...]).astype(BF16)

    def gate_up(f):
        u = u_ref[...]
        return (jnp.dot(u, wg_ref[:, f:f + tf], preferred_element_type=F32),
                jnp.dot(u, wu_ref[:, f:f + tf], preferred_element_type=F32))

    acc = h
    starts = list(range(0, dff, tf))
    nxt = gate_up(starts[0])
    for n, f in enumerate(starts):
        g, up = nxt
        if n + 1 < len(starts):
            nxt = gate_up(starts[n + 1])
        act = (g * _sigmoid(g) * up).astype(BF16)
        acc = acc + jnp.dot(act, wd_ref[f:f + tf, :], preferred_element_type=F32)
    o_ref[...] = acc


def _mlp(oa, ob, x2, w_out, n2, wg, wu, wd, tm, tf):
    t, d = x2.shape
    dff = wg.shape[1]
    row = lambda i: (i, 0)
    fixed = lambda i: (0, 0)
    resident = pl.Buffered(1)
    return pl.pallas_call(
        functools.partial(_mlp_kernel, tf=tf),
        grid=(t // tm,),
        in_specs=[
            pl.BlockSpec((tm, d), row), pl.BlockSpec((tm, d), row), pl.BlockSpec((tm, d), row),
            pl.BlockSpec((d, d), fixed, pipeline_mode=resident),
            pl.BlockSpec((1, d), fixed),
            pl.BlockSpec((d, dff), fixed, pipeline_mode=resident),
            pl.BlockSpec((d, dff), fixed, pipeline_mode=resident),
            pl.BlockSpec((dff, d), fixed, pipeline_mode=resident),
        ],
        out_specs=pl.BlockSpec((tm, d), row),
        out_shape=jax.ShapeDtypeStruct((t, d), F32),
        scratch_shapes=[pltpu.VMEM((tm, d), BF16)],
        compiler_params=pltpu.CompilerParams(
            dimension_semantics=("parallel",), vmem_limit_bytes=VMEM_LIMIT),
        name="mlp",
    )(oa, ob, x2, w_out, n2, wg, wu, wd)


def _layer(h2, batch, seq, norm1_w, w_in, conv_w, a_log, dt_bias, gdn_norm_w, q_norm_w, k_norm_w,
           lambda_q1, lambda_k1, lambda_q2, lambda_k2, subln_w, w_out, norm2_w, w_gate, w_up,
           w_down):
    d = h2.shape[1]
    heads = d // GDN_HEAD_DIM
    c0 = 4 * d
    c1 = c0 + 2 * heads
    w_lo = w_in[:, :c0].astype(BF16)
    w_hi = w_in[:, c1:].astype(BF16)
    w_ab = jnp.pad(w_in[:, c0:c1], ((0, 0), (0, LANES - 2 * heads))).astype(BF16)
    head_vecs = jnp.zeros((SUBLANES, LANES), F32)
    head_vecs = head_vecs.at[0, :heads].set(a_log).at[1, :heads].set(dt_bias)
    lam = jnp.zeros((SUBLANES, 2 * DIFF_HEAD_DIM), F32)
    lam = (lam.at[0, :DIFF_HEAD_DIM].set(lambda_q1).at[1, :DIFF_HEAD_DIM].set(lambda_k1)
           .at[2, :DIFF_HEAD_DIM].set(lambda_q2).at[3, :DIFF_HEAD_DIM].set(lambda_k2))
    qw = jnp.tile(q_norm_w, 2)[None, :]
    kw = jnp.tile(k_norm_w, 2)[None, :]

    proj, ab = _inproj(h2, norm1_w[None, :], w_lo, w_hi, w_ab, conv_w, qw, kw, tm=INPROJ_ROWS,
                       seq=seq)
    oa = _gdn(proj, ab, head_vecs, gdn_norm_w[None, :], batch, seq, heads)
    ob = _attn(proj, lam, subln_w[None, :], batch, seq, heads)
    return _mlp(oa, ob, h2, w_out.astype(BF16), norm2_w[None, :], w_gate.astype(BF16),
                w_up.astype(BF16), w_down.astype(BF16), tm=MLP_ROWS, tf=MLP_COLS)


def kernel(x, norm1_w, w_in, conv_w, a_log, dt_bias, gdn_norm_w, q_norm_w, k_norm_w, lambda_q1,
           lambda_k1, lambda_q2, lambda_k2, subln_w, w_out, norm2_w, w_gate, w_up, w_down):
    batch, seq, d = x.shape
    depth = norm1_w.shape[0]
    assert depth == 1, "lambda_init is specialised to a single layer"
    h2 = x.reshape(batch * seq, d)
    for l in range(depth):
        h2 = _layer(h2, batch, seq, norm1_w[l], w_in[l], conv_w[l], a_log[l], dt_bias[l],
                    gdn_norm_w[l], q_norm_w[l], k_norm_w[l], lambda_q1[l], lambda_k1[l],
                    lambda_q2[l], lambda_k2[l], subln_w[l], w_out[l], norm2_w[l], w_gate[l],
                    w_up[l], w_down[l])
    return h2.reshape(batch, seq, d)
```

```python
import functools
import math

import jax
import jax.numpy as jnp
from jax import lax
from jax.experimental import pallas as pl
from jax.experimental.pallas import tpu as pltpu

F32 = jnp.float32
BF16 = jnp.bfloat16

EPS = 1e-6
CONV_K = 4
GDN_HEAD_DIM = 128
DIFF_HEAD_DIM = 64
LAMBDA_INIT = 0.8 - 0.6 * math.exp(-0.3 * 0)

LANES = 128
MXU_COLS = 256
SUBLANES = 8
VMEM_LIMIT = 56 * 1024 * 1024

SEG_GQ, SEG_GK, SEG_GV, SEG_GZ, SEG_DQ, SEG_DK, SEG_DV, SEG_GATE_A, SEG_GATE_B = range(9)
NUM_SEG = 9

GDN_CHUNK = 128
GDN_ROWS = 1024
GDN_SKEW = 6
INPROJ_ROWS = 256
INPROJ_COLS = 256
INPROJ_PIECE = 64
INPROJ_LOOKAHEAD = 2
ATTN_BLOCK = 256
ATTN_LOOKAHEAD = 4
MLP_ROWS = 256
MLP_COLS = 256


def _bdot(a, b):
    return jnp.dot(a.astype(BF16), b.astype(BF16), preferred_element_type=F32)


def _bdot_nt(a, b):
    return lax.dot_general(a.astype(BF16), b.astype(BF16), (((1,), (1,)), ((), ())),
                           preferred_element_type=F32)


def _bdot_tn(a, b):
    return lax.dot_general(a.astype(BF16), b.astype(BF16), (((0,), (0,)), ((), ())),
                           preferred_element_type=F32)


def _sigmoid(x):
    return 1.0 / (1.0 + jnp.exp2(x * (-math.log2(math.e))))


def _half_rms(x, w, lo):
    x2 = x * x
    s_lo = jnp.sum(jnp.where(lo, x2, 0.0), axis=-1, keepdims=True)
    s_hi = jnp.sum(jnp.where(lo, 0.0, x2), axis=-1, keepdims=True)
    ms = jnp.where(lo, s_lo, s_hi) * (1.0 / DIFF_HEAD_DIM)
    return x * lax.rsqrt(ms + EPS) * w


def _inproj_kernel(x_ref, n1_ref, wlo_ref, whi_ref, wab_ref, cw_ref, qw_ref, kw_ref, proj_ref,
                   ab_ref, ybuf, halo, *, tm, d, tiles_per_seq):
    i = pl.program_id(0)
    pad = SUBLANES
    x = x_ref[...]
    ms = jnp.mean(x * x, axis=-1, keepdims=True)
    xn = (x * lax.rsqrt(ms + EPS) * n1_ref[...]).astype(BF16)
    ab_ref[...] = jnp.dot(xn, wab_ref[...], preferred_element_type=F32)
    seq_start = (i % tiles_per_seq) == 0
    lane = lax.broadcasted_iota(jnp.int32, (INPROJ_PIECE, LANES), 1)
    lo = lane < DIFF_HEAD_DIM
    q_scale = (DIFF_HEAD_DIM ** -0.5) * math.log2(math.e)

    def matmul(item):
        seg, cq = item
        col = seg * d + cq * INPROJ_COLS
        w_ref, wcol = (wlo_ref, col) if seg < SEG_DQ else (whi_ref, col - SEG_DQ * d)
        return jnp.dot(xn, w_ref[:, wcol:wcol + INPROJ_COLS], preferred_element_type=F32)

    def epilogue(item, y):
        seg, cq = item
        lc = cq * INPROJ_COLS
        col = seg * d + lc
        if seg in (SEG_GQ, SEG_GK, SEG_GV):
            hist = jnp.where(seq_start, 0.0, halo[seg, :, lc:lc + INPROJ_COLS])
            ybuf[seg, 0:pad, lc:lc + INPROJ_COLS] = hist
            ybuf[seg, pad:pad + tm, lc:lc + INPROJ_COLS] = y
            halo[seg, :, lc:lc + INPROJ_COLS] = y[tm - pad:tm, :]
            for c0 in range(0, INPROJ_COLS, GDN_HEAD_DIM):
                for r0 in range(0, tm, INPROJ_PIECE):
                    win = ybuf[seg, r0:r0 + pad + INPROJ_PIECE, lc + c0:lc + c0 + GDN_HEAD_DIM]
                    acc = None
                    for j in range(CONV_K):
                        shift = CONV_K - 1 - j
                        tap = pltpu.roll(win, shift, axis=0) if shift else win
                        term = tap[pad:, :] * cw_ref[j:j + 1, col + c0:col + c0 + GDN_HEAD_DIM]
                        acc = term if acc is None else acc + term
                    v = acc * _sigmoid(acc)
                    if seg != SEG_GV:
                        v = v * lax.rsqrt(jnp.sum(v * v, axis=-1, keepdims=True) + EPS)
                    if seg == SEG_GQ:
                        v = v * (GDN_HEAD_DIM ** -0.5)
                    proj_ref[r0:r0 + INPROJ_PIECE, col + c0:col + c0 + GDN_HEAD_DIM] = (
                        v.astype(BF16))
        elif seg == SEG_GZ:
            proj_ref[:, col:col + INPROJ_COLS] = (y * _sigmoid(y)).astype(BF16)
        elif seg in (SEG_DQ, SEG_DK):
            w_ref_n = qw_ref if seg == SEG_DQ else kw_ref
            for c0 in range(0, INPROJ_COLS, LANES):
                for r0 in range(0, tm, INPROJ_PIECE):
                    yn = _half_rms(y[r0:r0 + INPROJ_PIECE, c0:c0 + LANES], w_ref_n[...], lo)
                    if seg == SEG_DQ:
                        yn = yn * q_scale
                    proj_ref[r0:r0 + INPROJ_PIECE, col + c0:col + c0 + LANES] = yn.astype(BF16)
        elif seg == SEG_DV:
            proj_ref[:, col:col + INPROJ_COLS] = y.astype(BF16)
        else:
            proj_ref[:, col:col + INPROJ_COLS] = _sigmoid(y).astype(BF16)

    order = (SEG_GQ, SEG_DV, SEG_GK, SEG_GATE_A, SEG_GV, SEG_GATE_B, SEG_DQ, SEG_GZ, SEG_DK)
    items = [(seg, cq) for cq in range(d // INPROJ_COLS) for seg in order]
    ahead = [matmul(item) for item in items[:INPROJ_LOOKAHEAD]]
    for n, item in enumerate(items):
        y = ahead.pop(0)
        if n + INPROJ_LOOKAHEAD < len(items):
            ahead.append(matmul(items[n + INPROJ_LOOKAHEAD]))
        epilogue(item, y)


def _inproj(x2, n1, w_lo, w_hi, w_ab, conv_w, qw, kw, tm, seq):
    t, d = x2.shape
    fixed = lambda i: (0, 0)
    resident = pl.Buffered(1)
    kern = functools.partial(_inproj_kernel, tm=tm, d=d, tiles_per_seq=seq // tm)
    return pl.pallas_call(
        kern,
        grid=(t // tm,),
        in_specs=[
            pl.BlockSpec((tm, d), lambda i: (i, 0)),
            pl.BlockSpec((1, d), fixed),
            pl.BlockSpec(w_lo.shape, fixed, pipeline_mode=resident),
            pl.BlockSpec(w_hi.shape, fixed, pipeline_mode=resident),
            pl.BlockSpec((d, LANES), fixed),
            pl.BlockSpec((CONV_K, 3 * d), fixed),
            pl.BlockSpec((1, LANES), fixed),
            pl.BlockSpec((1, LANES), fixed),
        ],
        out_specs=[
            pl.BlockSpec((tm, NUM_SEG * d), lambda i: (i, 0)),
            pl.BlockSpec((tm, LANES), lambda i: (i, 0)),
        ],
        out_shape=[
            jax.ShapeDtypeStruct((t, NUM_SEG * d), BF16),
            jax.ShapeDtypeStruct((t, LANES), F32),
        ],
        scratch_shapes=[
            pltpu.VMEM((3, tm + SUBLANES, d), F32),
            pltpu.VMEM((3, SUBLANES, d), F32),
        ],
        compiler_params=pltpu.CompilerParams(
            dimension_semantics=("arbitrary",), vmem_limit_bytes=VMEM_LIMIT),
        name="inproj",
    )(x2, n1, w_lo, w_hi, w_ab, conv_w, qw, kw)


def _split3(x):
    h = x.astype(BF16)
    r = x - h.astype(F32)
    m = r.astype(BF16)
    l = (r - m.astype(F32)).astype(BF16)
    return h, m, l


def _gdn_kernel(q_ref, k_ref, v_ref, z_ref, ga_ref, ab_ref, hv_ref, nw_ref, o_ref, state,
                *, rows, chunk, heads):
    s = pl.program_id(1)

    @pl.when(s == 0)
    def _():
        state[...] = jnp.zeros_like(state)

    ri = lax.broadcasted_iota(jnp.int32, (chunk, chunk), 0)
    ci = lax.broadcasted_iota(jnp.int32, (chunk, chunk), 1)
    lower = ri >= ci
    strict = ri > ci
    eye = jnp.where(ri == ci, 1.0, 0.0).astype(F32)
    tri_ones = jnp.where(lower, 1.0, 0.0).astype(BF16)
    n_doublings = int(math.log2(chunk)) - 1
    a_neg = -jnp.exp(hv_ref[0:1, :])
    dt_bias = hv_ref[1:2, :]
    norm_w = nw_ref[...]

    hs = range(heads)
    cols = [h * GDN_HEAD_DIM for h in hs]

    def chunk_stages(r0):
        rows_c = slice(r0, r0 + chunk)
        ab = ab_ref[rows_c, :]
        xa = ab + dt_bias
        softplus = jnp.maximum(xa, 0.0) + jnp.log1p(jnp.exp(-jnp.abs(xa)))
        g = a_neg * softplus
        beta = _sigmoid(ab)
        g1, g2, g3 = _split3(g)
        gc = (jnp.dot(tri_ones, g1, preferred_element_type=F32)
              + jnp.dot(tri_ones, g2, preferred_element_type=F32)
              + jnp.dot(tri_ones, g3, preferred_element_type=F32))
        gct = gc.T
        q = [q_ref[rows_c, c0:c0 + GDN_HEAD_DIM].astype(F32) for c0 in cols]
        k = [k_ref[rows_c, c0:c0 + GDN_HEAD_DIM].astype(F32) for c0 in cols]
        v = [v_ref[rows_c, c0:c0 + GDN_HEAD_DIM].astype(F32) for c0 in cols]
        gcol = [jnp.broadcast_to(gc[:, h:h + 1], (chunk, GDN_HEAD_DIM)) for h in hs]
        bcol = [jnp.broadcast_to(beta[:, heads + h:heads + h + 1], (chunk, GDN_HEAD_DIM))
                for h in hs]
        glast = [jnp.broadcast_to(gc[chunk - 1:chunk, h:h + 1], (chunk, GDN_HEAD_DIM))
                 for h in hs]
        decay = [jnp.exp(jnp.where(lower, gcol[h] - jnp.broadcast_to(gct[h:h + 1, :],
                                                                     (chunk, chunk)), 0.0))
                 for h in hs]
        gram = [_bdot_nt(jnp.concatenate([k[h], q[h]], axis=0), k[h]) for h in hs]
        yield
        a_qk = [jnp.where(lower, gram[h][chunk:] * decay[h], 0.0) for h in hs]
        p = [jnp.where(strict, -(bcol[h] * gram[h][:chunk] * decay[h]), 0.0) for h in hs]
        t_inv = [eye + p[h] for h in hs]
        for _ in range(n_doublings):
            p = [_bdot(p[h], p[h]) for h in hs]
            yield
            t_inv = [t_inv[h] + _bdot(t_inv[h], p[h]) for h in hs]
            yield
        eg = [jnp.exp(gcol[h]) for h in hs]
        uw = [_bdot(t_inv[h], jnp.concatenate([v[h] * bcol[h], k[h] * bcol[h] * eg[h]], axis=1))
              for h in hs]
        yield
        st = [state[h] for h in hs]
        ws = [_bdot(jnp.concatenate([uw[h][:, GDN_HEAD_DIM:], q[h] * eg[h]], axis=0), st[h])
              for h in hs]
        yield
        v_new = [uw[h][:, :GDN_HEAD_DIM] - ws[h][:chunk] for h in hs]
        o = [ws[h][chunk:] + _bdot(a_qk[h], v_new[h]) for h in hs]
        for h in hs:
            k_dec = k[h] * jnp.exp(glast[h] - gcol[h])
            state[h] = st[h] * jnp.exp(glast[h]) + _bdot_tn(k_dec, v_new[h])
        yield
        for h in hs:
            c0 = cols[h]
            on = o[h] * lax.rsqrt(jnp.mean(o[h] * o[h], axis=-1, keepdims=True) + EPS) * norm_w
            z = z_ref[rows_c, c0:c0 + GDN_HEAD_DIM].astype(F32)
            gate = ga_ref[rows_c, c0:c0 + GDN_HEAD_DIM].astype(F32)
            o_ref[rows_c, c0:c0 + GDN_HEAD_DIM] = (on * z * gate).astype(BF16)
        yield

    n_stages = 2 * n_doublings + 5
    assert GDN_SKEW >= 2
    gens = [chunk_stages(c * chunk) for c in range(rows // chunk)]
    for step in range(GDN_SKEW * (len(gens) - 1) + n_stages):
        for c, gen in enumerate(gens):
            if 0 <= step - GDN_SKEW * c < n_stages:
                next(gen)


def _gdn(proj, ab, head_vecs, norm_w, batch, seq, heads):
    t = proj.shape[0]
    width = heads * GDN_HEAD_DIM
    rows = GDN_ROWS
    nsb = seq // rows

    def seg_spec(seg):
        return pl.BlockSpec((rows, width), lambda b, s, seg=seg: (b * nsb + s, seg))

    kern = functools.partial(_gdn_kernel, rows=rows, chunk=GDN_CHUNK, heads=heads)
    return pl.pallas_call(
        kern,
        grid=(batch, nsb),
        in_specs=[
            seg_spec(SEG_GQ), seg_spec(SEG_GK), seg_spec(SEG_GV), seg_spec(SEG_GZ),
            seg_spec(SEG_GATE_A),
            pl.BlockSpec((rows, LANES), lambda b, s: (b * nsb + s, 0)),
            pl.BlockSpec((SUBLANES, LANES), lambda b, s: (0, 0)),
            pl.BlockSpec((1, GDN_HEAD_DIM), lambda b, s: (0, 0)),
        ],
        out_specs=pl.BlockSpec((rows, width), lambda b, s: (b * nsb + s, 0)),
        out_shape=jax.ShapeDtypeStruct((t, width), BF16),
        scratch_shapes=[pltpu.VMEM((heads, GDN_HEAD_DIM, GDN_HEAD_DIM), F32)],
        compiler_params=pltpu.CompilerParams(
            dimension_semantics=("parallel", "arbitrary"), vmem_limit_bytes=VMEM_LIMIT),
        name="gdn",
    )(proj, proj, proj, proj, proj, ab, head_vecs, norm_w)


def _attn_kernel(q_ref, k_ref, v_ref, gate_ref, lam_ref, sw_ref, o_ref, qs_ref, vt_ref,
                 *, blk, seq):
    hd = 2 * DIFF_HEAD_DIM
    nblk = seq // blk
    lane = lax.broadcasted_iota(jnp.int32, (blk, hd), 1)
    lo = lane < DIFF_HEAD_DIM

    for i in range(nblk):
        rows = slice(i * blk, (i + 1) * blk)
        qn = q_ref[rows, :].astype(F32)
        qs_ref[0, rows, :] = jnp.where(lo, qn, 0.0).astype(BF16)
        qs_ref[1, rows, :] = jnp.where(lo, 0.0, qn).astype(BF16)
    for c in range(seq // hd):
        rows = slice(c * hd, (c + 1) * hd)
        vt_ref[:, rows] = v_ref[rows, :].astype(F32).T.astype(BF16)

    lp = lam_ref[...]
    lam = (jnp.exp(jnp.sum(lp[0:1, :] * lp[1:2, :], axis=-1, keepdims=True))
           - jnp.exp(jnp.sum(lp[2:3, :] * lp[3:4, :], axis=-1, keepdims=True)) + LAMBDA_INIT)
    kv_idx = lax.broadcasted_iota(jnp.int32, (blk, blk), 0)
    q_idx = lax.broadcasted_iota(jnp.int32, (blk, blk), 1)
    causal = kv_idx <= q_idx

    def scores(item):
        i, j, c = item
        return lax.dot_general(k_ref[j * blk:(j + 1) * blk, :],
                               qs_ref[c, i * blk:(i + 1) * blk, :], (((1,), (1,)), ((), ())),
                               preferred_element_type=F32)

    items = [(i, j, c) for i in range(nblk) for j in range(i + 1) for c in range(2)]
    ahead = [scores(items[n]) for n in range(ATTN_LOOKAHEAD)]
    stats = [None, None]
    for n, (i, j, c) in enumerate(items):
        sc = ahead.pop(0)
        if n + ATTN_LOOKAHEAD < len(items):
            ahead.append(scores(items[n + ATTN_LOOKAHEAD]))
        if j == i:
            sc = jnp.where(causal, sc, -jnp.inf)
        vt = vt_ref[:, j * blk:(j + 1) * blk]
        if j == 0:
            m = jnp.max(sc, axis=0, keepdims=True)
            p = jnp.exp2(sc - m)
            l = jnp.sum(p, axis=0, keepdims=True)
            acc = jnp.dot(vt, p.astype(BF16), preferred_element_type=F32)
        else:
            m, l, acc = stats[c]
            m_new = jnp.maximum(m, jnp.max(sc, axis=0, keepdims=True))
            alpha = jnp.exp2(m - m_new)
            p = jnp.exp2(sc - m_new)
            l = alpha * l + jnp.sum(p, axis=0, keepdims=True)
            acc = alpha * acc + jnp.dot(vt, p.astype(BF16), preferred_element_type=F32)
            m = m_new
        stats[c] = (m, l, acc)
        if j == i and c == 1:
            rows = slice(i * blk, (i + 1) * blk)
            on1 = stats[0][2] / stats[0][1]
            on2 = stats[1][2] / stats[1][1]
            o = (on1 - lam * on2).T
            o = o * lax.rsqrt(jnp.mean(o * o, axis=-1, keepdims=True) + EPS) * sw_ref[...]
            o = o * (1.0 - LAMBDA_INIT)
            o_ref[rows, :] = (o * gate_ref[rows, :].astype(F32)).astype(BF16)


def _attn(proj, lam, sw, batch, seq, heads):
    t = proj.shape[0]
    blk = ATTN_BLOCK
    hd = 2 * DIFF_HEAD_DIM
    kern = functools.partial(_attn_kernel, blk=blk, seq=seq)
    fixed = lambda b, h: (0, 0)
    return pl.pallas_call(
        kern,
        grid=(batch, heads),
        in_specs=[
            pl.BlockSpec((seq, hd), lambda b, h: (b, SEG_DQ * heads + h)),
            pl.BlockSpec((seq, hd), lambda b, h: (b, SEG_DK * heads + h)),
            pl.BlockSpec((seq, hd), lambda b, h: (b, SEG_DV * heads + h)),
            pl.BlockSpec((seq, hd), lambda b, h: (b, SEG_GATE_B * heads + h)),
            pl.BlockSpec((SUBLANES, hd), fixed),
            pl.BlockSpec((1, hd), fixed),
        ],
        out_specs=pl.BlockSpec((seq, hd), lambda b, h: (b, h)),
        out_shape=jax.ShapeDtypeStruct((t, heads * hd), BF16),
        scratch_shapes=[pltpu.VMEM((2, seq, hd), BF16), pltpu.VMEM((hd, seq), BF16)],
        compiler_params=pltpu.CompilerParams(
            dimension_semantics=("parallel", "parallel"), vmem_limit_bytes=VMEM_LIMIT),
        name="diffattn",
    )(proj, proj, proj, proj, lam, sw)


def _mlp_kernel(oa_ref, ob_ref, x_ref, wo_ref, n2_ref, wg_ref, wu_ref, wd_ref, o_ref, u_ref,
                *, tf):
    dff = wd_ref.shape[0]
    mixed =(oa_ref[...].astype(F32) + ob_ref[...].astype(F32)).astype(BF16)
    d = x_ref.shape[1]
    h = x_ref[...] + jnp.dot(mixed, wo_ref[:, :d], preferred_element_type=F32)
    ms = jnp.mean(h * h, axis=-1, keepdims=True)
    u_ref[...] = (h * lax.rsqrt(ms + EPS) * n2_ref[...]).astype(BF16)

    def gate_up(f):
        u = u_ref[...]
        return (jnp.dot(u, wg_ref[:, f:f + tf], preferred_element_type=F32),
                jnp.dot(u, wu_ref[:, f:f + tf], preferred_element_type=F32))

    acc = h
    starts = list(range(0, dff, tf))
    nxt = gate_up(starts[0])
    for n, f in enumerate(starts):
        g, up = nxt
        if n + 1 < len(starts):
            nxt = gate_up(starts[n + 1])
        act = (g * _sigmoid(g) * up).astype(BF16)
        acc = acc + jnp.dot(act, wd_ref[f:f + tf, :d], preferred_element_type=F32)
    o_ref[...] = acc


def _mlp(oa, ob, x2, w_out, n2, wg, wu, wd, tm, tf):
    t, d = x2.shape
    row = lambda i: (i, 0)
    fixed = lambda i: (0, 0)
    resident = pl.Buffered(1)
    return pl.pallas_call(
        functools.partial(_mlp_kernel, tf=tf),
        grid=(t // tm,),
        in_specs=[
            pl.BlockSpec((tm, d), row), pl.BlockSpec((tm, d), row), pl.BlockSpec((tm, d), row),
            pl.BlockSpec(w_out.shape, fixed, pipeline_mode=resident),
            pl.BlockSpec((1, d), fixed),
            pl.BlockSpec(wg.shape, fixed, pipeline_mode=resident),
            pl.BlockSpec(wu.shape, fixed, pipeline_mode=resident),
            pl.BlockSpec(wd.shape, fixed, pipeline_mode=resident),
        ],
        out_specs=pl.BlockSpec((tm, d), row),
        out_shape=jax.ShapeDtypeStruct((t, d), F32),
        scratch_shapes=[pltpu.VMEM((tm, d), BF16)],
        compiler_params=pltpu.CompilerParams(
            dimension_semantics=("parallel",), vmem_limit_bytes=VMEM_LIMIT),
        name="mlp",
    )(oa, ob, x2, w_out, n2, wg, wu, wd)


def _resident_weight(w):
    tiles = -(-w.shape[1] // MXU_COLS)
    width = (tiles + 1 - tiles % 2) * MXU_COLS
    return jnp.pad(w.astype(BF16), ((0, 0), (0, width - w.shape[1])))


def _layer(h2, batch, seq, norm1_w, w_in, conv_w, a_log, dt_bias, gdn_norm_w, q_norm_w, k_norm_w,
           lambda_q1, lambda_k1, lambda_q2, lambda_k2, subln_w, w_out, norm2_w, w_gate, w_up,
           w_down):
    d = h2.shape[1]
    heads = d // GDN_HEAD_DIM
    c0 = 4 * d
    c1 = c0 + 2 * heads
    w_lo = _resident_weight(w_in[:, :c0])
    w_hi = _resident_weight(w_in[:, c1:])
    w_ab = jnp.pad(w_in[:, c0:c1], ((0, 0), (0, LANES - 2 * heads))).astype(BF16)
    head_vecs = jnp.zeros((SUBLANES, LANES), F32)
    head_vecs = head_vecs.at[0, :heads].set(a_log).at[1, :heads].set(dt_bias)
    lam = jnp.zeros((SUBLANES, 2 * DIFF_HEAD_DIM), F32)
    lam = (lam.at[0, :DIFF_HEAD_DIM].set(lambda_q1).at[1, :DIFF_HEAD_DIM].set(lambda_k1)
           .at[2, :DIFF_HEAD_DIM].set(lambda_q2).at[3, :DIFF_HEAD_DIM].set(lambda_k2))
    qw = jnp.tile(q_norm_w, 2)[None, :]
    kw = jnp.tile(k_norm_w, 2)[None, :]

    proj, ab = _inproj(h2, norm1_w[None, :], w_lo, w_hi, w_ab, conv_w, qw, kw, tm=INPROJ_ROWS,
                       seq=seq)
    oa = _gdn(proj, ab, head_vecs, gdn_norm_w[None, :], batch, seq, heads)
    ob = _attn(proj, lam, subln_w[None, :], batch, seq, heads)
    return _mlp(oa, ob, h2, _resident_weight(w_out), norm2_w[None, :], _resident_weight(w_gate),
                _resident_weight(w_up), _resident_weight(w_down), tm=MLP_ROWS, tf=MLP_COLS)


def kernel(x, norm1_w, w_in, conv_w, a_log, dt_bias, gdn_norm_w, q_norm_w, k_norm_w, lambda_q1,
           lambda_k1, lambda_q2, lambda_k2, subln_w, w_out, norm2_w, w_gate, w_up, w_down):
    batch, seq, d = x.shape
    depth = norm1_w.shape[0]
    assert depth == 1, "lambda_init is specialised to a single layer"
    h2 = x.reshape(batch * seq, d)
    for l in range(depth):
        h2 = _layer(h2, batch, seq, norm1_w[l], w_in[l], conv_w[l], a_log[l], dt_bias[l],
                    gdn_norm_w[l], q_norm_w[l], k_norm_w[l], lambda_q1[l], lambda_k1[l],
                    lambda_q2[l], lambda_k2[l], subln_w[l], w_out[l], norm2_w[l], w_gate[l],
                    w_up[l], w_down[l])
    return h2.reshape(batch, seq, d)
```

```python
import functools
import math

import jax
import jax.numpy as jnp
from jax import lax
from jax.experimental import pallas as pl
from jax.experimental.pallas import tpu as pltpu

F32 = jnp.float32
BF16 = jnp.bfloat16

EPS = 1e-6
CONV_K = 4
GDN_HEAD_DIM = 128
DIFF_HEAD_DIM = 64
LAMBDA_INIT = 0.8 - 0.6 * math.exp(-0.3 * 0)

LANES = 128
MXU_COLS = 256
SUBLANES = 8
VMEM_LIMIT = 56 * 1024 * 1024

SEG_GQ, SEG_GK, SEG_GV, SEG_GZ, SEG_DQ, SEG_DK, SEG_DV, SEG_GATE_A, SEG_GATE_B = range(9)
NUM_SEG = 9
SEG_DQ2 = NUM_SEG
NUM_OUT_SEG = NUM_SEG + 1

GDN_CHUNK = 128
GDN_ROWS = 1024
GDN_SKEW = 6
INPROJ_ROWS = 256
INPROJ_COLS = 256
INPROJ_PIECE = 64
INPROJ_LOOKAHEAD = 2
ATTN_BLOCK = 256
ATTN_LOOKAHEAD = 4
ATTN_SUM_ROWS = 16
MLP_ROWS = 256
MLP_COLS = 256


def _bdot(a, b):
    return jnp.dot(a.astype(BF16), b.astype(BF16), preferred_element_type=F32)


def _bdot_nt(a, b):
    return lax.dot_general(a.astype(BF16), b.astype(BF16), (((1,), (1,)), ((), ())),
                           preferred_element_type=F32)


def _bdot_tn(a, b):
    return lax.dot_general(a.astype(BF16), b.astype(BF16), (((0,), (0,)), ((), ())),
                           preferred_element_type=F32)


def _sigmoid(x):
    return 1.0 / (1.0 + jnp.exp2(x * (-math.log2(math.e))))


def _half_rms(x, w, lo):
    x2 = x * x
    s_lo = jnp.sum(jnp.where(lo, x2, 0.0), axis=-1, keepdims=True)
    s_hi = jnp.sum(jnp.where(lo, 0.0, x2), axis=-1, keepdims=True)
    ms = jnp.where(lo, s_lo, s_hi) * (1.0 / DIFF_HEAD_DIM)
    return x * lax.rsqrt(ms + EPS) * w


def _inproj_kernel(x_ref, n1_ref, wlo_ref, whi_ref, wab_ref, cw_ref, qw_ref, kw_ref, proj_ref,
                   ab_ref, ybuf, halo, *, tm, d, tiles_per_seq):
    i = pl.program_id(0)
    pad = SUBLANES
    x = x_ref[...]
    ms = jnp.mean(x * x, axis=-1, keepdims=True)
    xn = (x * lax.rsqrt(ms + EPS) * n1_ref[...]).astype(BF16)
    ab_ref[...] = jnp.dot(xn, wab_ref[...], preferred_element_type=F32)
    seq_start = (i % tiles_per_seq) == 0
    lane = lax.broadcasted_iota(jnp.int32, (INPROJ_PIECE, LANES), 1)
    lo = lane < DIFF_HEAD_DIM
    q_scale = (DIFF_HEAD_DIM ** -0.5) * math.log2(math.e)

    def matmul(item):
        seg, cq = item
        col = seg * d + cq * INPROJ_COLS
        w_ref, wcol = (wlo_ref, col) if seg < SEG_DQ else (whi_ref, col - SEG_DQ * d)
        return jnp.dot(xn, w_ref[:, wcol:wcol + INPROJ_COLS], preferred_element_type=F32)

    def epilogue(item, y):
        seg, cq = item
        lc = cq * INPROJ_COLS
        col = seg * d + lc
        if seg in (SEG_GQ, SEG_GK, SEG_GV):
            hist = jnp.where(seq_start, 0.0, halo[seg, :, lc:lc + INPROJ_COLS])
            ybuf[seg, 0:pad, lc:lc + INPROJ_COLS] = hist
            ybuf[seg, pad:pad + tm, lc:lc + INPROJ_COLS] = y
            halo[seg, :, lc:lc + INPROJ_COLS] = y[tm - pad:tm, :]
            for c0 in range(0, INPROJ_COLS, GDN_HEAD_DIM):
                for r0 in range(0, tm, INPROJ_PIECE):
                    win = ybuf[seg, r0:r0 + pad + INPROJ_PIECE, lc + c0:lc + c0 + GDN_HEAD_DIM]
                    acc = None
                    for j in range(CONV_K):
                        shift = CONV_K - 1 - j
                        tap = pltpu.roll(win, shift, axis=0) if shift else win
                        term = tap[pad:, :] * cw_ref[j:j + 1, col + c0:col + c0 + GDN_HEAD_DIM]
                        acc = term if acc is None else acc + term
                    v = acc * _sigmoid(acc)
                    if seg != SEG_GV:
                        v = v * lax.rsqrt(jnp.sum(v * v, axis=-1, keepdims=True) + EPS)
                    if seg == SEG_GQ:
                        v = v * (GDN_HEAD_DIM ** -0.5)
                    proj_ref[r0:r0 + INPROJ_PIECE, col + c0:col + c0 + GDN_HEAD_DIM] = (
                        v.astype(BF16))
        elif seg == SEG_GZ:
            proj_ref[:, col:col + INPROJ_COLS] = (y * _sigmoid(y)).astype(BF16)
        elif seg in (SEG_DQ, SEG_DK):
            w_ref_n = qw_ref if seg == SEG_DQ else kw_ref
            for c0 in range(0, INPROJ_COLS, LANES):
                for r0 in range(0, tm, INPROJ_PIECE):
                    yn = _half_rms(y[r0:r0 + INPROJ_PIECE, c0:c0 + LANES], w_ref_n[...], lo)
                    rows_p = slice(r0, r0 + INPROJ_PIECE)
                    if seg == SEG_DQ:
                        yn = yn * q_scale
                        col2 = SEG_DQ2 * d + lc + c0
                        proj_ref[rows_p, col + c0:col + c0 + LANES] = (
                            jnp.where(lo, yn, 0.0).astype(BF16))
                        proj_ref[rows_p, col2:col2 + LANES] = jnp.where(lo, 0.0, yn).astype(BF16)
                    else:
                        proj_ref[rows_p, col + c0:col + c0 + LANES] = yn.astype(BF16)
        elif seg == SEG_DV:
            proj_ref[:, col:col + INPROJ_COLS] = y.astype(BF16)
        else:
            proj_ref[:, col:col + INPROJ_COLS] = _sigmoid(y).astype(BF16)

    order = (SEG_GQ, SEG_DV, SEG_GK, SEG_GATE_A, SEG_GV, SEG_GATE_B, SEG_DQ, SEG_GZ, SEG_DK)
    items = [(seg, cq) for cq in range(d // INPROJ_COLS) for seg in order]
    ahead = [matmul(item) for item in items[:INPROJ_LOOKAHEAD]]
    for n, item in enumerate(items):
        y = ahead.pop(0)
        if n + INPROJ_LOOKAHEAD < len(items):
            ahead.append(matmul(items[n + INPROJ_LOOKAHEAD]))
        epilogue(item, y)


def _inproj(x2, n1, w_lo, w_hi, w_ab, conv_w, qw, kw, tm, seq):
    t, d = x2.shape
    fixed = lambda i: (0, 0)
    resident = pl.Buffered(1)
    kern = functools.partial(_inproj_kernel, tm=tm, d=d, tiles_per_seq=seq // tm)
    return pl.pallas_call(
        kern,
        grid=(t // tm,),
        in_specs=[
            pl.BlockSpec((tm, d), lambda i: (i, 0)),
            pl.BlockSpec((1, d), fixed),
            pl.BlockSpec(w_lo.shape, fixed, pipeline_mode=resident),
            pl.BlockSpec(w_hi.shape, fixed, pipeline_mode=resident),
            pl.BlockSpec((d, LANES), fixed),
            pl.BlockSpec((CONV_K, 3 * d), fixed),
            pl.BlockSpec((1, LANES), fixed),
            pl.BlockSpec((1, LANES), fixed),
        ],
        out_specs=[
            pl.BlockSpec((tm, NUM_OUT_SEG * d), lambda i: (i, 0)),
            pl.BlockSpec((tm, LANES), lambda i: (i, 0)),
        ],
        out_shape=[
            jax.ShapeDtypeStruct((t, NUM_OUT_SEG * d), BF16),
            jax.ShapeDtypeStruct((t, LANES), F32),
        ],
        scratch_shapes=[
            pltpu.VMEM((3, tm + SUBLANES, d), F32),
            pltpu.VMEM((3, SUBLANES, d), F32),
        ],
        compiler_params=pltpu.CompilerParams(
            dimension_semantics=("arbitrary",), vmem_limit_bytes=VMEM_LIMIT),
        name="inproj",
    )(x2, n1, w_lo, w_hi, w_ab, conv_w, qw, kw)


def _split3(x):
    h = x.astype(BF16)
    r = x - h.astype(F32)
    m = r.astype(BF16)
    l = (r - m.astype(F32)).astype(BF16)
    return h, m, l


def _gdn_kernel(q_ref, k_ref, v_ref, z_ref, ga_ref, ab_ref, hv_ref, nw_ref, o_ref, state,
                *, rows, chunk, heads):
    s = pl.program_id(1)

    @pl.when(s == 0)
    def _():
        state[...] = jnp.zeros_like(state)

    ri = lax.broadcasted_iota(jnp.int32, (chunk, chunk), 0)
    ci = lax.broadcasted_iota(jnp.int32, (chunk, chunk), 1)
    lower = ri >= ci
    strict = ri > ci
    eye = jnp.where(ri == ci, 1.0, 0.0).astype(F32)
    tri_ones = jnp.where(lower, 1.0, 0.0).astype(BF16)
    n_doublings = int(math.log2(chunk)) - 1
    a_neg = -jnp.exp(hv_ref[0:1, :])
    dt_bias = hv_ref[1:2, :]
    norm_w = nw_ref[...]

    hs = range(heads)
    cols = [h * GDN_HEAD_DIM for h in hs]

    def chunk_stages(r0):
        rows_c = slice(r0, r0 + chunk)
        ab = ab_ref[rows_c, :]
        xa = ab + dt_bias
        softplus = jnp.maximum(xa, 0.0) + jnp.log1p(jnp.exp(-jnp.abs(xa)))
        g = a_neg * softplus
        beta = _sigmoid(ab)
        g1, g2, g3 = _split3(g)
        gc = (jnp.dot(tri_ones, g1, preferred_element_type=F32)
              + jnp.dot(tri_ones, g2, preferred_element_type=F32)
              + jnp.dot(tri_ones, g3, preferred_element_type=F32))
        gct = gc.T
        q = [q_ref[rows_c, c0:c0 + GDN_HEAD_DIM].astype(F32) for c0 in cols]
        k = [k_ref[rows_c, c0:c0 + GDN_HEAD_DIM].astype(F32) for c0 in cols]
        v = [v_ref[rows_c, c0:c0 + GDN_HEAD_DIM].astype(F32) for c0 in cols]
        gcol = [jnp.broadcast_to(gc[:, h:h + 1], (chunk, GDN_HEAD_DIM)) for h in hs]
        bcol = [jnp.broadcast_to(beta[:, heads + h:heads + h + 1], (chunk, GDN_HEAD_DIM))
                for h in hs]
        glast = [jnp.broadcast_to(gc[chunk - 1:chunk, h:h + 1], (chunk, GDN_HEAD_DIM))
                 for h in hs]
        decay = [jnp.exp(jnp.where(lower, gcol[h] - jnp.broadcast_to(gct[h:h + 1, :],
                                                                     (chunk, chunk)), 0.0))
                 for h in hs]
        gram = [_bdot_nt(jnp.concatenate([k[h], q[h]], axis=0), k[h]) for h in hs]
        yield
        a_qk = [jnp.where(lower, gram[h][chunk:] * decay[h], 0.0) for h in hs]
        p = [jnp.where(strict, -(bcol[h] * gram[h][:chunk] * decay[h]), 0.0) for h in hs]
        t_inv = [eye + p[h] for h in hs]
        for _ in range(n_doublings):
            p = [_bdot(p[h], p[h]) for h in hs]
            yield
            t_inv = [t_inv[h] + _bdot(t_inv[h], p[h]) for h in hs]
            yield
        eg = [jnp.exp(gcol[h]) for h in hs]
        uw = [_bdot(t_inv[h], jnp.concatenate([v[h] * bcol[h], k[h] * bcol[h] * eg[h]], axis=1))
              for h in hs]
        yield
        st = [state[h] for h in hs]
        ws = [_bdot(jnp.concatenate([uw[h][:, GDN_HEAD_DIM:], q[h] * eg[h]], axis=0), st[h])
              for h in hs]
        yield
        v_new = [uw[h][:, :GDN_HEAD_DIM] - ws[h][:chunk] for h in hs]
        o = [ws[h][chunk:] + _bdot(a_qk[h], v_new[h]) for h in hs]
        for h in hs:
            k_dec = k[h] * jnp.exp(glast[h] - gcol[h])
            state[h] = st[h] * jnp.exp(glast[h]) + _bdot_tn(k_dec, v_new[h])
        yield
        for h in hs:
            c0 = cols[h]
            on = o[h] * lax.rsqrt(jnp.mean(o[h] * o[h], axis=-1, keepdims=True) + EPS) * norm_w
            z = z_ref[rows_c, c0:c0 + GDN_HEAD_DIM].astype(F32)
            gate = ga_ref[rows_c, c0:c0 + GDN_HEAD_DIM].astype(F32)
            o_ref[rows_c, c0:c0 + GDN_HEAD_DIM] = (on * z * gate).astype(BF16)
        yield

    n_stages = 2 * n_doublings + 5
    assert GDN_SKEW >= 2
    gens = [chunk_stages(c * chunk) for c in range(rows // chunk)]
    for step in range(GDN_SKEW * (len(gens) - 1) + n_stages):
        for c, gen in enumerate(gens):
            if 0 <= step - GDN_SKEW * c < n_stages:
                next(gen)


def _gdn(proj, ab, head_vecs, norm_w, batch, seq, heads):
    t = proj.shape[0]
    width = heads * GDN_HEAD_DIM
    rows = GDN_ROWS
    nsb = seq // rows

    def seg_spec(seg):
        return pl.BlockSpec((rows, width), lambda b, s, seg=seg: (b * nsb + s, seg))

    kern = functools.partial(_gdn_kernel, rows=rows, chunk=GDN_CHUNK, heads=heads)
    return pl.pallas_call(
        kern,
        grid=(batch, nsb),
        in_specs=[
            seg_spec(SEG_GQ), seg_spec(SEG_GK), seg_spec(SEG_GV), seg_spec(SEG_GZ),
            seg_spec(SEG_GATE_A),
            pl.BlockSpec((rows, LANES), lambda b, s: (b * nsb + s, 0)),
            pl.BlockSpec((SUBLANES, LANES), lambda b, s: (0, 0)),
            pl.BlockSpec((1, GDN_HEAD_DIM), lambda b, s: (0, 0)),
        ],
        out_specs=pl.BlockSpec((rows, width), lambda b, s: (b * nsb + s, 0)),
        out_shape=jax.ShapeDtypeStruct((t, width), BF16),
        scratch_shapes=[pltpu.VMEM((heads, GDN_HEAD_DIM, GDN_HEAD_DIM), F32)],
        compiler_params=pltpu.CompilerParams(
            dimension_semantics=("parallel", "arbitrary"), vmem_limit_bytes=VMEM_LIMIT),
        name="gdn",
    )(proj, proj, proj, proj, proj, ab, head_vecs, norm_w)


def _attn_kernel(q1_ref, q2_ref, k_ref, v_ref, gate_ref, lam_ref, sw_ref, o_ref, vt_ref,
                 *, blk, seq):
    hd = 2 * DIFF_HEAD_DIM
    nblk = seq // blk
    q_refs = (q1_ref, q2_ref)

    for c in range(seq // hd):
        rows = slice(c * hd, (c + 1) * hd)
        vt_ref[0:hd, rows] = v_ref[rows, :].astype(F32).T.astype(BF16)
    ones_row = lax.broadcasted_iota(jnp.int32, (ATTN_SUM_ROWS, seq), 0) == 0
    vt_ref[hd:hd + ATTN_SUM_ROWS, :] = jnp.where(ones_row, 1.0, 0.0).astype(BF16)

    lp = lam_ref[...]
    lam = (jnp.exp(jnp.sum(lp[0:1, :] * lp[1:2, :], axis=-1, keepdims=True))
           - jnp.exp(jnp.sum(lp[2:3, :] * lp[3:4, :], axis=-1, keepdims=True)) + LAMBDA_INIT)
    kv_idx = lax.broadcasted_iota(jnp.int32, (blk, blk), 0)
    q_idx = lax.broadcasted_iota(jnp.int32, (blk, blk), 1)
    causal = kv_idx <= q_idx

    def scores(item):
        i, j, c = item
        return lax.dot_general(k_ref[j * blk:(j + 1) * blk, :],
                               q_refs[c][i * blk:(i + 1) * blk, :], (((1,), (1,)), ((), ())),
                               preferred_element_type=F32)

    items = [(i, j, c) for i in range(nblk) for j in range(i + 1) for c in range(2)]
    ahead = [scores(items[n]) for n in range(ATTN_LOOKAHEAD)]
    stats = [None, None]
    for n, (i, j, c) in enumerate(items):
        sc = ahead.pop(0)
        if n + ATTN_LOOKAHEAD < len(items):
            ahead.append(scores(items[n + ATTN_LOOKAHEAD]))
        if j == i:
            sc = jnp.where(causal, sc, -jnp.inf)
        vt = vt_ref[:, j * blk:(j + 1) * blk]
        if j == 0:
            m = jnp.max(sc, axis=0, keepdims=True)
            p = jnp.exp2(sc - m)
            acc = jnp.dot(vt, p.astype(BF16), preferred_element_type=F32)
        else:
            m, acc = stats[c]
            m_new = jnp.maximum(m, jnp.max(sc, axis=0, keepdims=True))
            alpha = jnp.exp2(m - m_new)
            p = jnp.exp2(sc - m_new)
            acc = alpha * acc + jnp.dot(vt, p.astype(BF16), preferred_element_type=F32)
            m = m_new
        stats[c] = (m, acc)
        if j == i and c == 1:
            rows = slice(i * blk, (i + 1) * blk)
            on1 = stats[0][1][:hd] / stats[0][1][hd:hd + 1]
            on2 = stats[1][1][:hd] / stats[1][1][hd:hd + 1]
            o = (on1 - lam * on2).T
            o = o * lax.rsqrt(jnp.mean(o * o, axis=-1, keepdims=True) + EPS) * sw_ref[...]
            o = o * (1.0 - LAMBDA_INIT)
            o_ref[rows, :] = (o * gate_ref[rows, :].astype(F32)).astype(BF16)


def _attn(proj, lam, sw, batch, seq, heads):
    t = proj.shape[0]
    blk = ATTN_BLOCK
    hd = 2 * DIFF_HEAD_DIM
    kern = functools.partial(_attn_kernel, blk=blk, seq=seq)
    fixed = lambda b, h: (0, 0)
    return pl.pallas_call(
        kern,
        grid=(batch, heads),
        in_specs=[
            pl.BlockSpec((seq, hd), lambda b, h: (b, SEG_DQ * heads + h)),
            pl.BlockSpec((seq, hd), lambda b, h: (b, SEG_DQ2 * heads + h)),
            pl.BlockSpec((seq, hd), lambda b, h: (b, SEG_DK * heads + h)),
            pl.BlockSpec((seq, hd), lambda b, h: (b, SEG_DV * heads + h)),
            pl.BlockSpec((seq, hd), lambda b, h: (b, SEG_GATE_B * heads + h)),
            pl.BlockSpec((SUBLANES, hd), fixed),
            pl.BlockSpec((1, hd), fixed),
        ],
        out_specs=pl.BlockSpec((seq, hd), lambda b, h: (b, h)),
        out_shape=jax.ShapeDtypeStruct((t, heads * hd), BF16),
        scratch_shapes=[pltpu.VMEM((hd + ATTN_SUM_ROWS, seq), BF16)],
        compiler_params=pltpu.CompilerParams(
            dimension_semantics=("parallel", "parallel"), vmem_limit_bytes=VMEM_LIMIT),
        name="diffattn",
    )(proj, proj, proj, proj, proj, lam, sw)


def _mlp_kernel(oa_ref, ob_ref, x_ref, wo_ref, n2_ref, wg_ref, wu_ref, wd_ref, o_ref, u_ref,
                *, tf):
    dff = wd_ref.shape[0]
    mixed =(oa_ref[...].astype(F32) + ob_ref[...].astype(F32)).astype(BF16)
    d = x_ref.shape[1]
    h = x_ref[...] + jnp.dot(mixed, wo_ref[:, :d], preferred_element_type=F32)
    ms = jnp.mean(h * h, axis=-1, keepdims=True)
    u_ref[...] = (h * lax.rsqrt(ms + EPS) * n2_ref[...]).astype(BF16)

    def gate_up(f):
        u = u_ref[...]
        return (jnp.dot(u, wg_ref[:, f:f + tf], preferred_element_type=F32),
                jnp.dot(u, wu_ref[:, f:f + tf], preferred_element_type=F32))

    acc = h
    starts = list(range(0, dff, tf))
    nxt = gate_up(starts[0])
    for n, f in enumerate(starts):
        g, up = nxt
        if n + 1 < len(starts):
            nxt = gate_up(starts[n + 1])
        act = (g * _sigmoid(g) * up).astype(BF16)
        acc = acc + jnp.dot(act, wd_ref[f:f + tf, :d], preferred_element_type=F32)
    o_ref[...] = acc


def _mlp(oa, ob, x2, w_out, n2, wg, wu, wd, tm, tf):
    t, d = x2.shape
    row = lambda i: (i, 0)
    fixed = lambda i: (0, 0)
    resident = pl.Buffered(1)
    return pl.pallas_call(
        functools.partial(_mlp_kernel, tf=tf),
        grid=(t // tm,),
        in_specs=[
            pl.BlockSpec((tm, d), row), pl.BlockSpec((tm, d), row), pl.BlockSpec((tm, d), row),
            pl.BlockSpec(w_out.shape, fixed, pipeline_mode=resident),
            pl.BlockSpec((1, d), fixed),
            pl.BlockSpec(wg.shape, fixed, pipeline_mode=resident),
            pl.BlockSpec(wu.shape, fixed, pipeline_mode=resident),
            pl.BlockSpec(wd.shape, fixed, pipeline_mode=resident),
        ],
        out_specs=pl.BlockSpec((tm, d), row),
        out_shape=jax.ShapeDtypeStruct((t, d), F32),
        scratch_shapes=[pltpu.VMEM((tm, d), BF16)],
        compiler_params=pltpu.CompilerParams(
            dimension_semantics=("parallel",), vmem_limit_bytes=VMEM_LIMIT),
        name="mlp",
    )(oa, ob, x2, w_out, n2, wg, wu, wd)


def _resident_weight(w):
    tiles = -(-w.shape[1] // MXU_COLS)
    width = (tiles + 1 - tiles % 2) * MXU_COLS
    return jnp.pad(w.astype(BF16), ((0, 0), (0, width - w.shape[1])))


def _layer(h2, batch, seq, norm1_w, w_in, conv_w, a_log, dt_bias, gdn_norm_w, q_norm_w, k_norm_w,
           lambda_q1, lambda_k1, lambda_q2, lambda_k2, subln_w, w_out, norm2_w, w_gate, w_up,
           w_down):
    d = h2.shape[1]
    heads = d // GDN_HEAD_DIM
    c0 = 4 * d
    c1 = c0 + 2 * heads
    w_lo = _resident_weight(w_in[:, :c0])
    w_hi = _resident_weight(w_in[:, c1:])
    w_ab = jnp.pad(w_in[:, c0:c1], ((0, 0), (0, LANES - 2 * heads))).astype(BF16)
    head_vecs = jnp.zeros((SUBLANES, LANES), F32)
    head_vecs = head_vecs.at[0, :heads].set(a_log).at[1, :heads].set(dt_bias)
    lam = jnp.zeros((SUBLANES, 2 * DIFF_HEAD_DIM), F32)
    lam = (lam.at[0, :DIFF_HEAD_DIM].set(lambda_q1).at[1, :DIFF_HEAD_DIM].set(lambda_k1)
           .at[2, :DIFF_HEAD_DIM].set(lambda_q2).at[3, :DIFF_HEAD_DIM].set(lambda_k2))
    qw = jnp.tile(q_norm_w, 2)[None, :]
    kw = jnp.tile(k_norm_w, 2)[None, :]

    proj, ab = _inproj(h2, norm1_w[None, :], w_lo, w_hi, w_ab, conv_w, qw, kw, tm=INPROJ_ROWS,
                       seq=seq)
    oa = _gdn(proj, ab, head_vecs, gdn_norm_w[None, :], batch, seq, heads)
    ob = _attn(proj, lam, subln_w[None, :], batch, seq, heads)
    return _mlp(oa, ob, h2, _resident_weight(w_out), norm2_w[None, :], _resident_weight(w_gate),
                _resident_weight(w_up), _resident_weight(w_down), tm=MLP_ROWS, tf=MLP_COLS)


def kernel(x, norm1_w, w_in, conv_w, a_log, dt_bias, gdn_norm_w, q_norm_w, k_norm_w, lambda_q1,
           lambda_k1, lambda_q2, lambda_k2, subln_w, w_out, norm2_w, w_gate, w_up, w_down):
    batch, seq, d = x.shape
    depth = norm1_w.shape[0]
    assert depth == 1, "lambda_init is specialised to a single layer"
    h2 = x.reshape(batch * seq, d)
    for l in range(depth):
        h2 = _layer(h2, batch, seq, norm1_w[l], w_in[l], conv_w[l], a_log[l], dt_bias[l],
                    gdn_norm_w[l], q_norm_w[l], k_norm_w[l], lambda_q1[l], lambda_k1[l],
                    lambda_q2[l], lambda_k2[l], subln_w[l], w_out[l], norm2_w[l], w_gate[l],
                    w_up[l], w_down[l])
    return h2.reshape(batch, seq, d)
```

```python
import functools
import math

import jax
import jax.numpy as jnp
from jax import lax
from jax.experimental import pallas as pl
from jax.experimental.pallas import tpu as pltpu

F32 = jnp.float32
BF16 = jnp.bfloat16

EPS = 1e-6
CONV_K = 4
GDN_HEAD_DIM = 128
DIFF_HEAD_DIM = 64
LAMBDA_INIT = 0.8 - 0.6 * math.exp(-0.3 * 0)

LANES = 128
MXU_COLS = 256
SUBLANES = 8
VMEM_LIMIT = 56 * 1024 * 1024

SEG_GQ, SEG_GK, SEG_GV, SEG_GZ, SEG_DQ, SEG_DK, SEG_DV, SEG_GATE_A, SEG_GATE_B = range(9)
NUM_SEG = 9
SEG_DQ2 = NUM_SEG
NUM_OUT_SEG = NUM_SEG + 1

GDN_CHUNK = 128
GDN_ROWS = 1024
GDN_GROUP = 4
GDN_SKEW = 3
INPROJ_ROWS = 256
INPROJ_COLS = 256
INPROJ_PIECE = 64
INPROJ_LOOKAHEAD = 2
ATTN_BLOCK = 256
ATTN_LOOKAHEAD = 4
ATTN_SUM_ROWS = 16
MLP_ROWS = 512
MLP_COLS = 256


def _bdot(a, b):
    return jnp.dot(a.astype(BF16), b.astype(BF16), preferred_element_type=F32)


def _bdot_nt(a, b):
    return lax.dot_general(a.astype(BF16), b.astype(BF16), (((1,), (1,)), ((), ())),
                           preferred_element_type=F32)


def _bdot_tn(a, b):
    return lax.dot_general(a.astype(BF16), b.astype(BF16), (((0,), (0,)), ((), ())),
                           preferred_element_type=F32)


def _sigmoid(x):
    return 1.0 / (1.0 + jnp.exp2(x * (-math.log2(math.e))))


def _half_rms(x, w, lo):
    x2 = x * x
    s_lo = jnp.sum(jnp.where(lo, x2, 0.0), axis=-1, keepdims=True)
    s_hi = jnp.sum(jnp.where(lo, 0.0, x2), axis=-1, keepdims=True)
    ms = jnp.where(lo, s_lo, s_hi) * (1.0 / DIFF_HEAD_DIM)
    return x * lax.rsqrt(ms + EPS) * w


def _inproj_kernel(x_ref, n1_ref, wlo_ref, whi_ref, wab_ref, cw_ref, qw_ref, kw_ref, proj_ref,
                   ab_ref, ybuf, halo, *, tm, d, tiles_per_seq):
    i = pl.program_id(0)
    pad = SUBLANES
    x = x_ref[...]
    ms = jnp.mean(x * x, axis=-1, keepdims=True)
    xn = (x * lax.rsqrt(ms + EPS) * n1_ref[...]).astype(BF16)
    ab_ref[...] = jnp.dot(xn, wab_ref[...], preferred_element_type=F32)
    seq_start = (i % tiles_per_seq) == 0
    lane = lax.broadcasted_iota(jnp.int32, (INPROJ_PIECE, LANES), 1)
    lo = lane < DIFF_HEAD_DIM
    q_scale = (DIFF_HEAD_DIM ** -0.5) * math.log2(math.e)

    def matmul(item):
        seg, cq = item
        col = seg * d + cq * INPROJ_COLS
        w_ref, wcol = (wlo_ref, col) if seg < SEG_DQ else (whi_ref, col - SEG_DQ * d)
        return jnp.dot(xn, w_ref[:, wcol:wcol + INPROJ_COLS], preferred_element_type=F32)

    def epilogue(item, y):
        seg, cq = item
        lc = cq * INPROJ_COLS
        col = seg * d + lc
        if seg in (SEG_GQ, SEG_GK, SEG_GV):
            hist = jnp.where(seq_start, 0.0, halo[seg, :, lc:lc + INPROJ_COLS])
            ybuf[seg, 0:pad, lc:lc + INPROJ_COLS] = hist
            ybuf[seg, pad:pad + tm, lc:lc + INPROJ_COLS] = y
            halo[seg, :, lc:lc + INPROJ_COLS] = y[tm - pad:tm, :]
            for c0 in range(0, INPROJ_COLS, GDN_HEAD_DIM):
                for r0 in range(0, tm, INPROJ_PIECE):
                    win = ybuf[seg, r0:r0 + pad + INPROJ_PIECE, lc + c0:lc + c0 + GDN_HEAD_DIM]
                    acc = None
                    for j in range(CONV_K):
                        shift = CONV_K - 1 - j
                        tap = pltpu.roll(win, shift, axis=0) if shift else win
                        term = tap[pad:, :] * cw_ref[j:j + 1, col + c0:col + c0 + GDN_HEAD_DIM]
                        acc = term if acc is None else acc + term
                    v = acc * _sigmoid(acc)
                    if seg != SEG_GV:
                        v = v * lax.rsqrt(jnp.sum(v * v, axis=-1, keepdims=True) + EPS)
                    if seg == SEG_GQ:
                        v = v * (GDN_HEAD_DIM ** -0.5)
                    proj_ref[r0:r0 + INPROJ_PIECE, col + c0:col + c0 + GDN_HEAD_DIM] = (
                        v.astype(BF16))
        elif seg == SEG_GZ:
            proj_ref[:, col:col + INPROJ_COLS] = (y * _sigmoid(y)).astype(BF16)
        elif seg in (SEG_DQ, SEG_DK):
            w_ref_n = qw_ref if seg == SEG_DQ else kw_ref
            for c0 in range(0, INPROJ_COLS, LANES):
                for r0 in range(0, tm, INPROJ_PIECE):
                    yn = _half_rms(y[r0:r0 + INPROJ_PIECE, c0:c0 + LANES], w_ref_n[...], lo)
                    rows_p = slice(r0, r0 + INPROJ_PIECE)
                    if seg == SEG_DQ:
                        yn = yn * q_scale
                        col2 = SEG_DQ2 * d + lc + c0
                        proj_ref[rows_p, col + c0:col + c0 + LANES] = (
                            jnp.where(lo, yn, 0.0).astype(BF16))
                        proj_ref[rows_p, col2:col2 + LANES] = jnp.where(lo, 0.0, yn).astype(BF16)
                    else:
                        proj_ref[rows_p, col + c0:col + c0 + LANES] = yn.astype(BF16)
        elif seg == SEG_DV:
            proj_ref[:, col:col + INPROJ_COLS] = y.astype(BF16)
        else:
            proj_ref[:, col:col + INPROJ_COLS] = _sigmoid(y).astype(BF16)

    order = (SEG_GQ, SEG_DV, SEG_GK, SEG_GATE_A, SEG_GV, SEG_GATE_B, SEG_DQ, SEG_GZ, SEG_DK)
    items = [(seg, cq) for cq in range(d // INPROJ_COLS) for seg in order]
    ahead = [matmul(item) for item in items[:INPROJ_LOOKAHEAD]]
    for n, item in enumerate(items):
        y = ahead.pop(0)
        if n + INPROJ_LOOKAHEAD < len(items):
            ahead.append(matmul(items[n + INPROJ_LOOKAHEAD]))
        epilogue(item, y)


def _inproj(x2, n1, w_lo, w_hi, w_ab, conv_w, qw, kw, tm, seq):
    t, d = x2.shape
    fixed = lambda i: (0, 0)
    resident = pl.Buffered(1)
    kern = functools.partial(_inproj_kernel, tm=tm, d=d, tiles_per_seq=seq // tm)
    return pl.pallas_call(
        kern,
        grid=(t // tm,),
        in_specs=[
            pl.BlockSpec((tm, d), lambda i: (i, 0)),
            pl.BlockSpec((1, d), fixed),
            pl.BlockSpec(w_lo.shape, fixed, pipeline_mode=resident),
            pl.BlockSpec(w_hi.shape, fixed, pipeline_mode=resident),
            pl.BlockSpec((d, LANES), fixed),
            pl.BlockSpec((CONV_K, 3 * d), fixed),
            pl.BlockSpec((1, LANES), fixed),
            pl.BlockSpec((1, LANES), fixed),
        ],
        out_specs=[
            pl.BlockSpec((tm, NUM_OUT_SEG * d), lambda i: (i, 0)),
            pl.BlockSpec((tm, LANES), lambda i: (i, 0)),
        ],
        out_shape=[
            jax.ShapeDtypeStruct((t, NUM_OUT_SEG * d), BF16),
            jax.ShapeDtypeStruct((t, LANES), F32),
        ],
        scratch_shapes=[
            pltpu.VMEM((3, tm + SUBLANES, d), F32),
            pltpu.VMEM((3, SUBLANES, d), F32),
        ],
        compiler_params=pltpu.CompilerParams(
            dimension_semantics=("arbitrary",), vmem_limit_bytes=VMEM_LIMIT),
        name="inproj",
    )(x2, n1, w_lo, w_hi, w_ab, conv_w, qw, kw)


def _split3(x):
    h = x.astype(BF16)
    r = x - h.astype(F32)
    m = r.astype(BF16)
    l = (r - m.astype(F32)).astype(BF16)
    return h, m, l


def _gdn_kernel(q_ref, k_ref, v_ref, z_ref, ga_ref, ab_ref, hv_ref, nw_ref, o_ref, state,
                *, rows, chunk, heads):
    s = pl.program_id(1)

    @pl.when(s == 0)
    def _():
        state[...] = jnp.zeros_like(state)

    ri = lax.broadcasted_iota(jnp.int32, (chunk, chunk), 0)
    ci = lax.broadcasted_iota(jnp.int32, (chunk, chunk), 1)
    lower = ri >= ci
    strict = ri > ci
    eye = jnp.where(ri == ci, 1.0, 0.0).astype(F32)
    tri_ones = jnp.where(lower, 1.0, 0.0).astype(BF16)
    block_sizes = [2 ** e for e in range(1, int(math.log2(chunk)))]
    pair_mask = strict & ((ri >> 1) == (ci >> 1))
    couple_masks = [((ri >> (b.bit_length())) == (ci >> (b.bit_length())))
                    & ((ri & b) != 0) & ((ci & b) == 0) for b in block_sizes]
    a_neg = -jnp.exp(hv_ref[0:1, :])
    dt_bias = hv_ref[1:2, :]
    norm_w = nw_ref[...]

    cols = [h * GDN_HEAD_DIM for h in range(heads)]
    shared = {}

    def unit_stages(r0, hs):
        rows_c = slice(r0, r0 + chunk)
        if r0 not in shared:
            ab = ab_ref[rows_c, :]
            xa = ab + dt_bias
            softplus = jnp.maximum(xa, 0.0) + jnp.log1p(jnp.exp(-jnp.abs(xa)))
            g = a_neg * softplus
            beta = _sigmoid(ab)
            g1, g2, g3 = _split3(g)
            gc = (jnp.dot(tri_ones, g1, preferred_element_type=F32)
                  + jnp.dot(tri_ones, g2, preferred_element_type=F32)
                  + jnp.dot(tri_ones, g3, preferred_element_type=F32))
            shared[r0] = (gc, gc.T, beta)
        gc, gct, beta = shared[r0]
        q = {h: q_ref[rows_c, cols[h]:cols[h] + GDN_HEAD_DIM].astype(F32) for h in hs}
        k = {h: k_ref[rows_c, cols[h]:cols[h] + GDN_HEAD_DIM].astype(F32) for h in hs}
        v = {h: v_ref[rows_c, cols[h]:cols[h] + GDN_HEAD_DIM].astype(F32) for h in hs}
        gcol = {h: jnp.broadcast_to(gc[:, h:h + 1], (chunk, GDN_HEAD_DIM)) for h in hs}
        bcol = {h: jnp.broadcast_to(beta[:, heads + h:heads + h + 1], (chunk, GDN_HEAD_DIM))
                for h in hs}
        glast = {h: jnp.broadcast_to(gc[chunk - 1:chunk, h:h + 1], (chunk, GDN_HEAD_DIM))
                 for h in hs}
        decay = {h: jnp.exp(jnp.where(lower, gcol[h] - jnp.broadcast_to(gct[h:h + 1, :],
                                                                     (chunk, chunk)), 0.0))
                 for h in hs}
        gram = {h: _bdot_nt(jnp.concatenate([k[h], q[h]], axis=0), k[h]) for h in hs}
        yield
        a_qk = {h: jnp.where(lower, gram[h][chunk:] * decay[h], 0.0) for h in hs}
        neg_a = {h: jnp.where(strict, -(bcol[h] * gram[h][:chunk] * decay[h]), 0.0) for h in hs}
        t_inv = {h: eye + jnp.where(pair_mask, neg_a[h], 0.0) for h in hs}
        for couple_mask in couple_masks:
            x = {h: _bdot(jnp.where(couple_mask, neg_a[h], 0.0), t_inv[h]) for h in hs}
            yield
            t_inv = {h: t_inv[h] + _bdot(t_inv[h], x[h]) for h in hs}
            yield
        eg = {h: jnp.exp(gcol[h]) for h in hs}
        uw = {h: _bdot(t_inv[h], jnp.concatenate([v[h] * bcol[h], k[h] * bcol[h] * eg[h]], axis=1))
              for h in hs}
        yield
        st = {h: state[h] for h in hs}
        ws = {h: _bdot(jnp.concatenate([uw[h][:, GDN_HEAD_DIM:], q[h] * eg[h]], axis=0), st[h])
              for h in hs}
        yield
        v_new = {h: uw[h][:, :GDN_HEAD_DIM] - ws[h][:chunk] for h in hs}
        o = {h: ws[h][chunk:] + _bdot(a_qk[h], v_new[h]) for h in hs}
        for h in hs:
            k_dec = k[h] * jnp.exp(glast[h] - gcol[h])
            state[h] = st[h] * jnp.exp(glast[h]) + _bdot_tn(k_dec, v_new[h])
        yield
        for h in hs:
            c0 = cols[h]
            on = o[h] * lax.rsqrt(jnp.mean(o[h] * o[h], axis=-1, keepdims=True) + EPS) * norm_w
            z = z_ref[rows_c, c0:c0 + GDN_HEAD_DIM].astype(F32)
            gate = ga_ref[rows_c, c0:c0 + GDN_HEAD_DIM].astype(F32)
            o_ref[rows_c, c0:c0 + GDN_HEAD_DIM] = (on * z * gate).astype(BF16)
        yield

    n_stages = 2 * len(block_sizes) + 5
    groups = [list(range(g0, g0 + GDN_GROUP)) for g0 in range(0, heads, GDN_GROUP)]
    assert GDN_SKEW * len(groups) >= 2
    gens = [unit_stages(c * chunk, hs) for c in range(rows // chunk) for hs in groups]
    for step in range(GDN_SKEW * (len(gens) - 1) + n_stages):
        for u, gen in enumerate(gens):
            if 0 <= step - GDN_SKEW * u < n_stages:
                next(gen)


def _gdn(proj, ab, head_vecs, norm_w, batch, seq, heads):
    t = proj.shape[0]
    width = heads * GDN_HEAD_DIM
    rows = GDN_ROWS
    nsb = seq // rows

    def seg_spec(seg):
        return pl.BlockSpec((rows, width), lambda b, s, seg=seg: (b * nsb + s, seg))

    kern = functools.partial(_gdn_kernel, rows=rows, chunk=GDN_CHUNK, heads=heads)
    return pl.pallas_call(
        kern,
        grid=(batch, nsb),
        in_specs=[
            seg_spec(SEG_GQ), seg_spec(SEG_GK), seg_spec(SEG_GV), seg_spec(SEG_GZ),
            seg_spec(SEG_GATE_A),
            pl.BlockSpec((rows, LANES), lambda b, s: (b * nsb + s, 0)),
            pl.BlockSpec((SUBLANES, LANES), lambda b, s: (0, 0)),
            pl.BlockSpec((1, GDN_HEAD_DIM), lambda b, s: (0, 0)),
        ],
        out_specs=pl.BlockSpec((rows, width), lambda b, s: (b * nsb + s, 0)),
        out_shape=jax.ShapeDtypeStruct((t, width), BF16),
        scratch_shapes=[pltpu.VMEM((heads, GDN_HEAD_DIM, GDN_HEAD_DIM), F32)],
        compiler_params=pltpu.CompilerParams(
            dimension_semantics=("parallel", "arbitrary"), vmem_limit_bytes=VMEM_LIMIT),
        name="gdn",
    )(proj, proj, proj, proj, proj, ab, head_vecs, norm_w)


def _attn_kernel(q1_ref, q2_ref, k_ref, v_ref, gate_ref, lam_ref, sw_ref, o_ref, vt_ref,
                 *, blk, seq):
    hd = 2 * DIFF_HEAD_DIM
    nblk = seq // blk
    q_refs = (q1_ref, q2_ref)

    for c in range(seq // hd):
        rows = slice(c * hd, (c + 1) * hd)
        vt_ref[0:hd, rows] = v_ref[rows, :].astype(F32).T.astype(BF16)
    ones_row = lax.broadcasted_iota(jnp.int32, (ATTN_SUM_ROWS, seq), 0) == 0
    vt_ref[hd:hd + ATTN_SUM_ROWS, :] = jnp.where(ones_row, 1.0, 0.0).astype(BF16)

    lp = lam_ref[...]
    lam = (jnp.exp(jnp.sum(lp[0:1, :] * lp[1:2, :], axis=-1, keepdims=True))
           - jnp.exp(jnp.sum(lp[2:3, :] * lp[3:4, :], axis=-1, keepdims=True)) + LAMBDA_INIT)
    kv_idx = lax.broadcasted_iota(jnp.int32, (blk, blk), 0)
    q_idx = lax.broadcasted_iota(jnp.int32, (blk, blk), 1)
    causal = kv_idx <= q_idx

    def scores(item):
        i, j, c = item
        return lax.dot_general(k_ref[j * blk:(j + 1) * blk, :],
                               q_refs[c][i * blk:(i + 1) * blk, :], (((1,), (1,)), ((), ())),
                               preferred_element_type=F32)

    items = [(i, j, c) for i in range(nblk) for j in range(i + 1) for c in range(2)]
    ahead = [scores(items[n]) for n in range(ATTN_LOOKAHEAD)]
    stats = [None, None]
    for n, (i, j, c) in enumerate(items):
        sc = ahead.pop(0)
        if n + ATTN_LOOKAHEAD < len(items):
            ahead.append(scores(items[n + ATTN_LOOKAHEAD]))
        if j == i:
            sc = jnp.where(causal, sc, -jnp.inf)
        vt = vt_ref[:, j * blk:(j + 1) * blk]
        if j == 0:
            m = jnp.max(sc, axis=0, keepdims=True)
            p = jnp.exp2(sc - m)
            acc = jnp.dot(vt, p.astype(BF16), preferred_element_type=F32)
        else:
            m, acc = stats[c]
            m_new = jnp.maximum(m, jnp.max(sc, axis=0, keepdims=True))
            alpha = jnp.exp2(m - m_new)
            p = jnp.exp2(sc - m_new)
            acc = alpha * acc + jnp.dot(vt, p.astype(BF16), preferred_element_type=F32)
            m = m_new
        stats[c] = (m, acc)
        if j == i and c == 1:
            rows = slice(i * blk, (i + 1) * blk)
            on1 = stats[0][1][:hd] / stats[0][1][hd:hd + 1]
            on2 = stats[1][1][:hd] / stats[1][1][hd:hd + 1]
            o = (on1 - lam * on2).T
            o = o * lax.rsqrt(jnp.mean(o * o, axis=-1, keepdims=True) + EPS) * sw_ref[...]
            o = o * (1.0 - LAMBDA_INIT)
            o_ref[rows, :] = (o * gate_ref[rows, :].astype(F32)).astype(BF16)


def _attn(proj, lam, sw, batch, seq, heads):
    t = proj.shape[0]
    blk = ATTN_BLOCK
    hd = 2 * DIFF_HEAD_DIM
    kern = functools.partial(_attn_kernel, blk=blk, seq=seq)
    fixed = lambda b, h: (0, 0)
    return pl.pallas_call(
        kern,
        grid=(batch, heads),
        in_specs=[
            pl.BlockSpec((seq, hd), lambda b, h: (b, SEG_DQ * heads + h)),
            pl.BlockSpec((seq, hd), lambda b, h: (b, SEG_DQ2 * heads + h)),
            pl.BlockSpec((seq, hd), lambda b, h: (b, SEG_DK * heads + h)),
            pl.BlockSpec((seq, hd), lambda b, h: (b, SEG_DV * heads + h)),
            pl.BlockSpec((seq, hd), lambda b, h: (b, SEG_GATE_B * heads + h)),
            pl.BlockSpec((SUBLANES, hd), fixed),
            pl.BlockSpec((1, hd), fixed),
        ],
        out_specs=pl.BlockSpec((seq, hd), lambda b, h: (b, h)),
        out_shape=jax.ShapeDtypeStruct((t, heads * hd), BF16),
        scratch_shapes=[pltpu.VMEM((hd + ATTN_SUM_ROWS, seq), BF16)],
        compiler_params=pltpu.CompilerParams(
            dimension_semantics=("parallel", "parallel"), vmem_limit_bytes=VMEM_LIMIT),
        name="diffattn",
    )(proj, proj, proj, proj, proj, lam, sw)


def _mlp_kernel(oa_ref, ob_ref, x_ref, wo_ref, n2_ref, wg_ref, wu_ref, wd_ref, o_ref, u_ref,
                *, tf):
    dff = wd_ref.shape[0]
    mixed =(oa_ref[...].astype(F32) + ob_ref[...].astype(F32)).astype(BF16)
    d = x_ref.shape[1]
    h = x_ref[...] + jnp.dot(mixed, wo_ref[:, :d], preferred_element_type=F32)
    ms = jnp.mean(h * h, axis=-1, keepdims=True)
    u_ref[...] = (h * lax.rsqrt(ms + EPS) * n2_ref[...]).astype(BF16)

    def gate_up(f):
        u = u_ref[...]
        return (jnp.dot(u, wg_ref[:, f:f + tf], preferred_element_type=F32),
                jnp.dot(u, wu_ref[:, f:f + tf], preferred_element_type=F32))

    acc = h
    starts = list(range(0, dff, tf))
    nxt = gate_up(starts[0])
    for n, f in enumerate(starts):
        g, up = nxt
        if n + 1 < len(starts):
            nxt = gate_up(starts[n + 1])
        act = (g * _sigmoid(g) * up).astype(BF16)
        acc = acc + jnp.dot(act, wd_ref[f:f + tf, :d], preferred_element_type=F32)
    o_ref[...] = acc


def _mlp(oa, ob, x2, w_out, n2, wg, wu, wd, tm, tf):
    t, d = x2.shape
    row = lambda i: (i, 0)
    fixed = lambda i: (0, 0)
    resident = pl.Buffered(1)
    return pl.pallas_call(
        functools.partial(_mlp_kernel, tf=tf),
        grid=(t // tm,),
        in_specs=[
            pl.BlockSpec((tm, d), row), pl.BlockSpec((tm, d), row), pl.BlockSpec((tm, d), row),
            pl.BlockSpec(w_out.shape, fixed, pipeline_mode=resident),
            pl.BlockSpec((1, d), fixed),
            pl.BlockSpec(wg.shape, fixed, pipeline_mode=resident),
            pl.BlockSpec(wu.shape, fixed, pipeline_mode=resident),
            pl.BlockSpec(wd.shape, fixed, pipeline_mode=resident),
        ],
        out_specs=pl.BlockSpec((tm, d), row),
        out_shape=jax.ShapeDtypeStruct((t, d), F32),
        scratch_shapes=[pltpu.VMEM((tm, d), BF16)],
        compiler_params=pltpu.CompilerParams(
            dimension_semantics=("parallel",), vmem_limit_bytes=VMEM_LIMIT),
        name="mlp",
    )(oa, ob, x2, w_out, n2, wg, wu, wd)


def _resident_weight(w):
    tiles = -(-w.shape[1] // MXU_COLS)
    width = (tiles + 1 - tiles % 2) * MXU_COLS
    return jnp.pad(w.astype(BF16), ((0, 0), (0, width - w.shape[1])))


def _layer(h2, batch, seq, norm1_w, w_in, conv_w, a_log, dt_bias, gdn_norm_w, q_norm_w, k_norm_w,
           lambda_q1, lambda_k1, lambda_q2, lambda_k2, subln_w, w_out, norm2_w, w_gate, w_up,
           w_down):
    d = h2.shape[1]
    heads = d // GDN_HEAD_DIM
    c0 = 4 * d
    c1 = c0 + 2 * heads
    w_lo = _resident_weight(w_in[:, :c0])
    w_hi = _resident_weight(w_in[:, c1:])
    w_ab = jnp.pad(w_in[:, c0:c1], ((0, 0), (0, LANES - 2 * heads))).astype(BF16)
    head_vecs = jnp.zeros((SUBLANES, LANES), F32)
    head_vecs = head_vecs.at[0, :heads].set(a_log).at[1, :heads].set(dt_bias)
    lam = jnp.zeros((SUBLANES, 2 * DIFF_HEAD_DIM), F32)
    lam = (lam.at[0, :DIFF_HEAD_DIM].set(lambda_q1).at[1, :DIFF_HEAD_DIM].set(lambda_k1)
           .at[2, :DIFF_HEAD_DIM].set(lambda_q2).at[3, :DIFF_HEAD_DIM].set(lambda_k2))
    qw = jnp.tile(q_norm_w, 2)[None, :]
    kw = jnp.tile(k_norm_w, 2)[None, :]

    proj, ab = _inproj(h2, norm1_w[None, :], w_lo, w_hi, w_ab, conv_w, qw, kw, tm=INPROJ_ROWS,
                       seq=seq)
    oa = _gdn(proj, ab, head_vecs, gdn_norm_w[None, :], batch, seq, heads)
    ob = _attn(proj, lam, subln_w[None, :], batch, seq, heads)
    return _mlp(oa, ob, h2, _resident_weight(w_out), norm2_w[None, :], _resident_weight(w_gate),
                _resident_weight(w_up), _resident_weight(w_down), tm=MLP_ROWS, tf=MLP_COLS)


def kernel(x, norm1_w, w_in, conv_w, a_log, dt_bias, gdn_norm_w, q_norm_w, k_norm_w, lambda_q1,
           lambda_k1, lambda_q2, lambda_k2, subln_w, w_out, norm2_w, w_gate, w_up, w_down):
    batch, seq, d = x.shape
    depth = norm1_w.shape[0]
    assert depth == 1, "lambda_init is specialised to a single layer"
    h2 = x.reshape(batch * seq, d)
    for l in range(depth):
        h2 = _layer(h2, batch, seq, norm1_w[l], w_in[l], conv_w[l], a_log[l], dt_bias[l],
                    gdn_norm_w[l], q_norm_w[l], k_norm_w[l], lambda_q1[l], lambda_k1[l],
                    lambda_q2[l], lambda_k2[l], subln_w[l], w_out[l], norm2_w[l], w_gate[l],
                    w_up[l], w_down[l])
    return h2.reshape(batch, seq, d)
```

```python
import functools
import math

import jax
import jax.numpy as jnp
from jax import lax
from jax.experimental import pallas as pl
from jax.experimental.pallas import tpu as pltpu

F32 = jnp.float32
BF16 = jnp.bfloat16

EPS = 1e-6
CONV_K = 4
GDN_HEAD_DIM = 128
DIFF_HEAD_DIM = 64
LAMBDA_INIT = 0.8 - 0.6 * math.exp(-0.3 * 0)

LANES = 128
MXU_COLS = 256
SUBLANES = 8
VMEM_LIMIT = 56 * 1024 * 1024

SEG_GQ, SEG_GK, SEG_GV, SEG_GZ, SEG_DQ, SEG_DK, SEG_DV, SEG_GATE_A, SEG_GATE_B = range(9)
NUM_SEG = 9
SEG_DQ2 = NUM_SEG
NUM_OUT_SEG = NUM_SEG + 1

GDN_CHUNK = 128
GDN_ROWS = 1024
GDN_GROUP = 4
GDN_SKEW = 3
INPROJ_ROWS = 256
INPROJ_COLS = 256
INPROJ_PIECE = 64
INPROJ_LOOKAHEAD = 2
ATTN_BLOCK = 256
ATTN_LOOKAHEAD = 4
ATTN_SUM_ROWS = 16
MLP_ROWS = 512
MLP_COLS = 256


def _bdot(a, b):
    return jnp.dot(a.astype(BF16), b.astype(BF16), preferred_element_type=F32)


def _bdot_nt(a, b):
    return lax.dot_general(a.astype(BF16), b.astype(BF16), (((1,), (1,)), ((), ())),
                           preferred_element_type=F32)


def _bdot_tn(a, b):
    return lax.dot_general(a.astype(BF16), b.astype(BF16), (((0,), (0,)), ((), ())),
                           preferred_element_type=F32)


def _sigmoid(x):
    return 1.0 / (1.0 + jnp.exp2(x * (-math.log2(math.e))))


def _half_rms(x, w, lo):
    x2 = x * x
    s_lo = jnp.sum(jnp.where(lo, x2, 0.0), axis=-1, keepdims=True)
    s_hi = jnp.sum(jnp.where(lo, 0.0, x2), axis=-1, keepdims=True)
    ms = jnp.where(lo, s_lo, s_hi) * (1.0 / DIFF_HEAD_DIM)
    return x * lax.rsqrt(ms + EPS) * w


def _inproj_kernel(x_ref, n1_ref, w_ref, wab_ref, cw_ref, qw_ref, kw_ref, proj_ref, ab_ref,
                   ybuf, halo, *, tm, d, tiles_per_seq):
    i = pl.program_id(0)
    pad = SUBLANES
    x = x_ref[...]
    ms = jnp.mean(x * x, axis=-1, keepdims=True)
    xn = (x * lax.rsqrt(ms + EPS) * n1_ref[...]).astype(BF16)
    ab_ref[...] = jnp.dot(xn, wab_ref[...], preferred_element_type=F32)
    seq_start = (i % tiles_per_seq) == 0
    lane = lax.broadcasted_iota(jnp.int32, (INPROJ_PIECE, LANES), 1)
    lo = lane < DIFF_HEAD_DIM
    q_scale = (DIFF_HEAD_DIM ** -0.5) * math.log2(math.e)

    def matmul(item):
        seg, cq = item
        col = seg * d + cq * INPROJ_COLS
        wcol = col if seg < SEG_DQ else col + MXU_COLS
        return jnp.dot(xn, w_ref[:, wcol:wcol + INPROJ_COLS], preferred_element_type=F32)

    def epilogue(item, y):
        seg, cq = item
        lc = cq * INPROJ_COLS
        col = seg * d + lc
        if seg in (SEG_GQ, SEG_GK, SEG_GV):
            hist = jnp.where(seq_start, 0.0, halo[seg, :, lc:lc + INPROJ_COLS])
            ybuf[seg, 0:pad, lc:lc + INPROJ_COLS] = hist
            ybuf[seg, pad:pad + tm, lc:lc + INPROJ_COLS] = y
            halo[seg, :, lc:lc + INPROJ_COLS] = y[tm - pad:tm, :]
            for c0 in range(0, INPROJ_COLS, GDN_HEAD_DIM):
                for r0 in range(0, tm, INPROJ_PIECE):
                    win = ybuf[seg, r0:r0 + pad + INPROJ_PIECE, lc + c0:lc + c0 + GDN_HEAD_DIM]
                    acc = None
                    for j in range(CONV_K):
                        shift = CONV_K - 1 - j
                        tap = pltpu.roll(win, shift, axis=0) if shift else win
                        term = tap[pad:, :] * cw_ref[j:j + 1, col + c0:col + c0 + GDN_HEAD_DIM]
                        acc = term if acc is None else acc + term
                    v = acc * _sigmoid(acc)
                    if seg != SEG_GV:
                        v = v * lax.rsqrt(jnp.sum(v * v, axis=-1, keepdims=True) + EPS)
                    if seg == SEG_GQ:
                        v = v * (GDN_HEAD_DIM ** -0.5)
                    proj_ref[r0:r0 + INPROJ_PIECE, col + c0:col + c0 + GDN_HEAD_DIM] = (
                        v.astype(BF16))
        elif seg == SEG_GZ:
            proj_ref[:, col:col + INPROJ_COLS] = (y * _sigmoid(y)).astype(BF16)
        elif seg in (SEG_DQ, SEG_DK):
            w_ref_n = qw_ref if seg == SEG_DQ else kw_ref
            for c0 in range(0, INPROJ_COLS, LANES):
                for r0 in range(0, tm, INPROJ_PIECE):
                    yn = _half_rms(y[r0:r0 + INPROJ_PIECE, c0:c0 + LANES], w_ref_n[...], lo)
                    rows_p = slice(r0, r0 + INPROJ_PIECE)
                    if seg == SEG_DQ:
                        yn = yn * q_scale
                        col2 = SEG_DQ2 * d + lc + c0
                        proj_ref[rows_p, col + c0:col + c0 + LANES] = (
                            jnp.where(lo, yn, 0.0).astype(BF16))
                        proj_ref[rows_p, col2:col2 + LANES] = jnp.where(lo, 0.0, yn).astype(BF16)
                    else:
                        proj_ref[rows_p, col + c0:col + c0 + LANES] = yn.astype(BF16)
        elif seg == SEG_DV:
            proj_ref[:, col:col + INPROJ_COLS] = y.astype(BF16)
        else:
            proj_ref[:, col:col + INPROJ_COLS] = _sigmoid(y).astype(BF16)

    order = (SEG_GQ, SEG_DV, SEG_GK, SEG_GATE_A, SEG_GV, SEG_GATE_B, SEG_DQ, SEG_GZ, SEG_DK)
    items = [(seg, cq) for cq in range(d // INPROJ_COLS) for seg in order]
    ahead = [matmul(item) for item in items[:INPROJ_LOOKAHEAD]]
    for n, item in enumerate(items):
        y = ahead.pop(0)
        if n + INPROJ_LOOKAHEAD < len(items):
            ahead.append(matmul(items[n + INPROJ_LOOKAHEAD]))
        epilogue(item, y)


def _inproj(x2, n1, w_main, w_ab, conv_w, qw, kw, tm, seq):
    t, d = x2.shape
    fixed = lambda i: (0, 0)
    resident = pl.Buffered(1)
    kern = functools.partial(_inproj_kernel, tm=tm, d=d, tiles_per_seq=seq // tm)
    return pl.pallas_call(
        kern,
        grid=(t // tm,),
        in_specs=[
            pl.BlockSpec((tm, d), lambda i: (i, 0)),
            pl.BlockSpec((1, d), fixed),
            pl.BlockSpec(w_main.shape, fixed, pipeline_mode=resident),
            pl.BlockSpec((d, LANES), fixed),
            pl.BlockSpec((CONV_K, 3 * d), fixed),
            pl.BlockSpec((1, LANES), fixed),
            pl.BlockSpec((1, LANES), fixed),
        ],
        out_specs=[
            pl.BlockSpec((tm, NUM_OUT_SEG * d), lambda i: (i, 0)),
            pl.BlockSpec((tm, LANES), lambda i: (i, 0)),
        ],
        out_shape=[
            jax.ShapeDtypeStruct((t, NUM_OUT_SEG * d), BF16),
            jax.ShapeDtypeStruct((t, LANES), F32),
        ],
        scratch_shapes=[
            pltpu.VMEM((3, tm + SUBLANES, d), F32),
            pltpu.VMEM((3, SUBLANES, d), F32),
        ],
        compiler_params=pltpu.CompilerParams(
            dimension_semantics=("arbitrary",), vmem_limit_bytes=VMEM_LIMIT),
        name="inproj",
    )(x2, n1, w_main, w_ab, conv_w, qw, kw)


def _split3(x):
    h = x.astype(BF16)
    r = x - h.astype(F32)
    m = r.astype(BF16)
    l = (r - m.astype(F32)).astype(BF16)
    return h, m, l


def _gdn_kernel(q_ref, k_ref, v_ref, z_ref, ga_ref, ab_ref, hv_ref, nw_ref, o_ref, state,
                *, rows, chunk, heads):
    s = pl.program_id(1)

    @pl.when(s == 0)
    def _():
        state[...] = jnp.zeros_like(state)

    ri = lax.broadcasted_iota(jnp.int32, (chunk, chunk), 0)
    ci = lax.broadcasted_iota(jnp.int32, (chunk, chunk), 1)
    lower = ri >= ci
    strict = ri > ci
    eye = jnp.where(ri == ci, 1.0, 0.0).astype(F32)
    tri_ones = jnp.where(lower, 1.0, 0.0).astype(BF16)
    block_sizes = [2 ** e for e in range(1, int(math.log2(chunk)))]
    pair_mask = strict & ((ri >> 1) == (ci >> 1))
    couple_masks = [((ri >> (b.bit_length())) == (ci >> (b.bit_length())))
                    & ((ri & b) != 0) & ((ci & b) == 0) for b in block_sizes]
    a_neg = -jnp.exp(hv_ref[0:1, :])
    dt_bias = hv_ref[1:2, :]
    norm_w = nw_ref[...]

    cols = [h * GDN_HEAD_DIM for h in range(heads)]
    shared = {}

    def unit_stages(r0, hs):
        rows_c = slice(r0, r0 + chunk)
        if r0 not in shared:
            ab = ab_ref[rows_c, :]
            xa = ab + dt_bias
            softplus = jnp.maximum(xa, 0.0) + jnp.log1p(jnp.exp(-jnp.abs(xa)))
            g = a_neg * softplus
            beta = _sigmoid(ab)
            g1, g2, g3 = _split3(g)
            gc = (jnp.dot(tri_ones, g1, preferred_element_type=F32)
                  + jnp.dot(tri_ones, g2, preferred_element_type=F32)
                  + jnp.dot(tri_ones, g3, preferred_element_type=F32))
            shared[r0] = (gc, gc.T, beta)
        gc, gct, beta = shared[r0]
        q = {h: q_ref[rows_c, cols[h]:cols[h] + GDN_HEAD_DIM].astype(F32) for h in hs}
        k = {h: k_ref[rows_c, cols[h]:cols[h] + GDN_HEAD_DIM].astype(F32) for h in hs}
        v = {h: v_ref[rows_c, cols[h]:cols[h] + GDN_HEAD_DIM].astype(F32) for h in hs}
        gcol = {h: jnp.broadcast_to(gc[:, h:h + 1], (chunk, GDN_HEAD_DIM)) for h in hs}
        bcol = {h: jnp.broadcast_to(beta[:, heads + h:heads + h + 1], (chunk, GDN_HEAD_DIM))
                for h in hs}
        glast = {h: jnp.broadcast_to(gc[chunk - 1:chunk, h:h + 1], (chunk, GDN_HEAD_DIM))
                 for h in hs}
        decay = {h: jnp.exp(jnp.where(lower, gcol[h] - jnp.broadcast_to(gct[h:h + 1, :],
                                                                     (chunk, chunk)), 0.0))
                 for h in hs}
        gram = {h: _bdot_nt(jnp.concatenate([k[h], q[h]], axis=0), k[h]) for h in hs}
        yield
        a_qk = {h: jnp.where(lower, gram[h][chunk:] * decay[h], 0.0) for h in hs}
        neg_a = {h: jnp.where(strict, -(bcol[h] * gram[h][:chunk] * decay[h]), 0.0) for h in hs}
        t_inv = {h: eye + jnp.where(pair_mask, neg_a[h], 0.0) for h in hs}
        for couple_mask in couple_masks:
            x = {h: _bdot(jnp.where(couple_mask, neg_a[h], 0.0), t_inv[h]) for h in hs}
            yield
            t_inv = {h: t_inv[h] + _bdot(t_inv[h], x[h]) for h in hs}
            yield
        eg = {h: jnp.exp(gcol[h]) for h in hs}
        uw = {h: _bdot(t_inv[h], jnp.concatenate([v[h] * bcol[h], k[h] * bcol[h] * eg[h]], axis=1))
              for h in hs}
        yield
        st = {h: state[h] for h in hs}
        ws = {h: _bdot(jnp.concatenate([uw[h][:, GDN_HEAD_DIM:], q[h] * eg[h]], axis=0), st[h])
              for h in hs}
        yield
        v_new = {h: uw[h][:, :GDN_HEAD_DIM] - ws[h][:chunk] for h in hs}
        o = {h: ws[h][chunk:] + _bdot(a_qk[h], v_new[h]) for h in hs}
        for h in hs:
            k_dec = k[h] * jnp.exp(glast[h] - gcol[h])
            state[h] = st[h] * jnp.exp(glast[h]) + _bdot_tn(k_dec, v_new[h])
        yield
        for h in hs:
            c0 = cols[h]
            on = o[h] * lax.rsqrt(jnp.mean(o[h] * o[h], axis=-1, keepdims=True) + EPS) * norm_w
            z = z_ref[rows_c, c0:c0 + GDN_HEAD_DIM].astype(F32)
            gate = ga_ref[rows_c, c0:c0 + GDN_HEAD_DIM].astype(F32)
            o_ref[rows_c, c0:c0 + GDN_HEAD_DIM] = (on * z * gate).astype(BF16)
        yield

    n_stages = 2 * len(block_sizes) + 5
    groups = [list(range(g0, g0 + GDN_GROUP)) for g0 in range(0, heads, GDN_GROUP)]
    assert GDN_SKEW * len(groups) >= 2
    gens = [unit_stages(c * chunk, hs) for c in range(rows // chunk) for hs in groups]
    for step in range(GDN_SKEW * (len(gens) - 1) + n_stages):
        for u, gen in enumerate(gens):
            if 0 <= step - GDN_SKEW * u < n_stages:
                next(gen)


def _gdn(proj, ab, head_vecs, norm_w, batch, seq, heads):
    t = proj.shape[0]
    width = heads * GDN_HEAD_DIM
    rows = GDN_ROWS
    nsb = seq // rows

    def seg_spec(seg):
        return pl.BlockSpec((rows, width), lambda b, s, seg=seg: (b * nsb + s, seg))

    kern = functools.partial(_gdn_kernel, rows=rows, chunk=GDN_CHUNK, heads=heads)
    return pl.pallas_call(
        kern,
        grid=(batch, nsb),
        in_specs=[
            seg_spec(SEG_GQ), seg_spec(SEG_GK), seg_spec(SEG_GV), seg_spec(SEG_GZ),
            seg_spec(SEG_GATE_A),
            pl.BlockSpec((rows, LANES), lambda b, s: (b * nsb + s, 0)),
            pl.BlockSpec((SUBLANES, LANES), lambda b, s: (0, 0)),
            pl.BlockSpec((1, GDN_HEAD_DIM), lambda b, s: (0, 0)),
        ],
        out_specs=pl.BlockSpec((rows, width), lambda b, s: (b * nsb + s, 0)),
        out_shape=jax.ShapeDtypeStruct((t, width), BF16),
        scratch_shapes=[pltpu.VMEM((heads, GDN_HEAD_DIM, GDN_HEAD_DIM), F32)],
        compiler_params=pltpu.CompilerParams(
            dimension_semantics=("parallel", "arbitrary"), vmem_limit_bytes=VMEM_LIMIT),
        name="gdn",
    )(proj, proj, proj, proj, proj, ab, head_vecs, norm_w)


def _attn_kernel(q1_ref, q2_ref, k_ref, v_ref, gate_ref, lam_ref, sw_ref, o_ref, vt_ref,
                 *, blk, seq):
    hd = 2 * DIFF_HEAD_DIM
    nblk = seq // blk
    q_refs = (q1_ref, q2_ref)

    for c in range(seq // hd):
        rows = slice(c * hd, (c + 1) * hd)
        vt_ref[0:hd, rows] = v_ref[rows, :].astype(F32).T.astype(BF16)
    ones_row = lax.broadcasted_iota(jnp.int32, (ATTN_SUM_ROWS, seq), 0) == 0
    vt_ref[hd:hd + ATTN_SUM_ROWS, :] = jnp.where(ones_row, 1.0, 0.0).astype(BF16)

    lp = lam_ref[...]
    lam = (jnp.exp(jnp.sum(lp[0:1, :] * lp[1:2, :], axis=-1, keepdims=True))
           - jnp.exp(jnp.sum(lp[2:3, :] * lp[3:4, :], axis=-1, keepdims=True)) + LAMBDA_INIT)
    kv_idx = lax.broadcasted_iota(jnp.int32, (blk, blk), 0)
    q_idx = lax.broadcasted_iota(jnp.int32, (blk, blk), 1)
    causal = kv_idx <= q_idx

    def scores(item):
        i, j, c = item
        return lax.dot_general(k_ref[j * blk:(j + 1) * blk, :],
                               q_refs[c][i * blk:(i + 1) * blk, :], (((1,), (1,)), ((), ())),
                               preferred_element_type=F32)

    items = [(i, j, c) for i in range(nblk) for j in range(i + 1) for c in range(2)]
    ahead = [scores(items[n]) for n in range(ATTN_LOOKAHEAD)]
    stats = [None, None]
    for n, (i, j, c) in enumerate(items):
        sc = ahead.pop(0)
        if n + ATTN_LOOKAHEAD < len(items):
            ahead.append(scores(items[n + ATTN_LOOKAHEAD]))
        if j == i:
            sc = jnp.where(causal, sc, -jnp.inf)
        vt = vt_ref[:, j * blk:(j + 1) * blk]
        if j == 0:
            m = jnp.max(sc, axis=0, keepdims=True)
            p = jnp.exp2(sc - m)
            acc = jnp.dot(vt, p.astype(BF16), preferred_element_type=F32)
        else:
            m, acc = stats[c]
            m_new = jnp.maximum(m, jnp.max(sc, axis=0, keepdims=True))
            alpha = jnp.exp2(m - m_new)
            p = jnp.exp2(sc - m_new)
            acc = alpha * acc + jnp.dot(vt, p.astype(BF16), preferred_element_type=F32)
            m = m_new
        stats[c] = (m, acc)
        if j == i and c == 1:
            rows = slice(i * blk, (i + 1) * blk)
            on1 = stats[0][1][:hd] / stats[0][1][hd:hd + 1]
            on2 = stats[1][1][:hd] / stats[1][1][hd:hd + 1]
            o = (on1 - lam * on2).T
            o = o * lax.rsqrt(jnp.mean(o * o, axis=-1, keepdims=True) + EPS) * sw_ref[...]
            o = o * (1.0 - LAMBDA_INIT)
            o_ref[rows, :] = (o * gate_ref[rows, :].astype(F32)).astype(BF16)


def _attn(proj, lam, sw, batch, seq, heads):
    t = proj.shape[0]
    blk = ATTN_BLOCK
    hd = 2 * DIFF_HEAD_DIM
    kern = functools.partial(_attn_kernel, blk=blk, seq=seq)
    fixed = lambda b, h: (0, 0)
    return pl.pallas_call(
        kern,
        grid=(batch, heads),
        in_specs=[
            pl.BlockSpec((seq, hd), lambda b, h: (b, SEG_DQ * heads + h)),
            pl.BlockSpec((seq, hd), lambda b, h: (b, SEG_DQ2 * heads + h)),
            pl.BlockSpec((seq, hd), lambda b, h: (b, SEG_DK * heads + h)),
            pl.BlockSpec((seq, hd), lambda b, h: (b, SEG_DV * heads + h)),
            pl.BlockSpec((seq, hd), lambda b, h: (b, SEG_GATE_B * heads + h)),
            pl.BlockSpec((SUBLANES, hd), fixed),
            pl.BlockSpec((1, hd), fixed),
        ],
        out_specs=pl.BlockSpec((seq, hd), lambda b, h: (b, h)),
        out_shape=jax.ShapeDtypeStruct((t, heads * hd), BF16),
        scratch_shapes=[pltpu.VMEM((hd + ATTN_SUM_ROWS, seq), BF16)],
        compiler_params=pltpu.CompilerParams(
            dimension_semantics=("parallel", "parallel"), vmem_limit_bytes=VMEM_LIMIT),
        name="diffattn",
    )(proj, proj, proj, proj, proj, lam, sw)


def _mlp_kernel(oa_ref, ob_ref, x_ref, wo_ref, n2_ref, wg_ref, wu_ref, wd_ref, o_ref, u_ref,
                *, tf):
    dff = wd_ref.shape[0]
    mixed =(oa_ref[...].astype(F32) + ob_ref[...].astype(F32)).astype(BF16)
    d = x_ref.shape[1]
    h = x_ref[...] + jnp.dot(mixed, wo_ref[:, :d], preferred_element_type=F32)
    ms = jnp.mean(h * h, axis=-1, keepdims=True)
    u_ref[...] = (h * lax.rsqrt(ms + EPS) * n2_ref[...]).astype(BF16)

    def gate_up(f):
        u = u_ref[...]
        return (jnp.dot(u, wg_ref[:, f:f + tf], preferred_element_type=F32),
                jnp.dot(u, wu_ref[:, f:f + tf], preferred_element_type=F32))

    acc = h
    starts = list(range(0, dff, tf))
    nxt = gate_up(starts[0])
    for n, f in enumerate(starts):
        g, up = nxt
        if n + 1 < len(starts):
            nxt = gate_up(starts[n + 1])
        act = (g * _sigmoid(g) * up).astype(BF16)
        acc = acc + jnp.dot(act, wd_ref[f:f + tf, :d], preferred_element_type=F32)
    o_ref[...] = acc


def _mlp(oa, ob, x2, w_out, n2, wg, wu, wd, tm, tf):
    t, d = x2.shape
    row = lambda i: (i, 0)
    fixed = lambda i: (0, 0)
    resident = pl.Buffered(1)
    return pl.pallas_call(
        functools.partial(_mlp_kernel, tf=tf),
        grid=(t // tm,),
        in_specs=[
            pl.BlockSpec((tm, d), row), pl.BlockSpec((tm, d), row), pl.BlockSpec((tm, d), row),
            pl.BlockSpec(w_out.shape, fixed, pipeline_mode=resident),
            pl.BlockSpec((1, d), fixed),
            pl.BlockSpec(wg.shape, fixed, pipeline_mode=resident),
            pl.BlockSpec(wu.shape, fixed, pipeline_mode=resident),
            pl.BlockSpec(wd.shape, fixed, pipeline_mode=resident),
        ],
        out_specs=pl.BlockSpec((tm, d), row),
        out_shape=jax.ShapeDtypeStruct((t, d), F32),
        scratch_shapes=[pltpu.VMEM((tm, d), BF16)],
        compiler_params=pltpu.CompilerParams(
            dimension_semantics=("parallel",), vmem_limit_bytes=VMEM_LIMIT),
        name="mlp",
    )(oa, ob, x2, w_out, n2, wg, wu, wd)


def _resident_weight(w):
    tiles = -(-w.shape[1] // MXU_COLS)
    width = (tiles + 1 - tiles % 2) * MXU_COLS
    return jnp.pad(w.astype(BF16), ((0, 0), (0, width - w.shape[1])))


def _layer(h2, batch, seq, norm1_w, w_in, conv_w, a_log, dt_bias, gdn_norm_w, q_norm_w, k_norm_w,
           lambda_q1, lambda_k1, lambda_q2, lambda_k2, subln_w, w_out, norm2_w, w_gate, w_up,
           w_down):
    d = h2.shape[1]
    heads = d // GDN_HEAD_DIM
    c0 = 4 * d
    c1 = c0 + 2 * heads
    gap = jnp.zeros((d, MXU_COLS), w_in.dtype)
    w_main = _resident_weight(jnp.concatenate([w_in[:, :c0], gap, w_in[:, c1:]], axis=1))
    w_ab = jnp.pad(w_in[:, c0:c1], ((0, 0), (0, LANES - 2 * heads))).astype(BF16)
    head_vecs = jnp.zeros((SUBLANES, LANES), F32)
    head_vecs = head_vecs.at[0, :heads].set(a_log).at[1, :heads].set(dt_bias)
    lam = jnp.zeros((SUBLANES, 2 * DIFF_HEAD_DIM), F32)
    lam = (lam.at[0, :DIFF_HEAD_DIM].set(lambda_q1).at[1, :DIFF_HEAD_DIM].set(lambda_k1)
           .at[2, :DIFF_HEAD_DIM].set(lambda_q2).at[3, :DIFF_HEAD_DIM].set(lambda_k2))
    qw = jnp.tile(q_norm_w, 2)[None, :]
    kw = jnp.tile(k_norm_w, 2)[None, :]

    proj, ab = _inproj(h2, norm1_w[None, :], w_main, w_ab, conv_w, qw, kw, tm=INPROJ_ROWS, seq=seq)
    oa = _gdn(proj, ab, head_vecs, gdn_norm_w[None, :], batch, seq, heads)
    ob = _attn(proj, lam, subln_w[None, :], batch, seq, heads)
    return _mlp(oa, ob, h2, _resident_weight(w_out), norm2_w[None, :], _resident_weight(w_gate),
                _resident_weight(w_up), _resident_weight(w_down), tm=MLP_ROWS, tf=MLP_COLS)


def kernel(x, norm1_w, w_in, conv_w, a_log, dt_bias, gdn_norm_w, q_norm_w, k_norm_w, lambda_q1,
           lambda_k1, lambda_q2, lambda_k2, subln_w, w_out, norm2_w, w_gate, w_up, w_down):
    batch, seq, d = x.shape
    depth = norm1_w.shape[0]
    assert depth == 1, "lambda_init is specialised to a single layer"
    h2 = x.reshape(batch * seq, d)
    for l in range(depth):
        h2 = _layer(h2, batch, seq, norm1_w[l], w_in[l], conv_w[l], a_log[l], dt_bias[l],
                    gdn_norm_w[l], q_norm_w[l], k_norm_w[l], lambda_q1[l], lambda_k1[l],
                    lambda_q2[l], lambda_k2[l], subln_w[l], w_out[l], norm2_w[l], w_gate[l],
                    w_up[l], w_down[l])
    return h2.reshape(batch, seq, d)
```

```python
import functools
import math

import jax
import jax.numpy as jnp
from jax import lax
from jax.experimental import pallas as pl
from jax.experimental.pallas import tpu as pltpu

F32 = jnp.float32
BF16 = jnp.bfloat16

EPS = 1e-6
CONV_K = 4
GDN_HEAD_DIM = 128
DIFF_HEAD_DIM = 64
LAMBDA_INIT = 0.8 - 0.6 * math.exp(-0.3 * 0)

LANES = 128
MXU_COLS = 256
SUBLANES = 8
VMEM_LIMIT = 56 * 1024 * 1024

SEG_GQ, SEG_GK, SEG_GV, SEG_GZ, SEG_DQ, SEG_DK, SEG_DV, SEG_GATE_A, SEG_GATE_B = range(9)
NUM_SEG = 9
SEG_DQ2 = NUM_SEG
NUM_OUT_SEG = NUM_SEG + 1

GDN_CHUNK = 128
GDN_ROWS = 1024
GDN_GROUP = 4
GDN_SKEW = 3
INPROJ_ROWS = 256
INPROJ_COLS = 256
INPROJ_PIECE = 64
INPROJ_LOOKAHEAD = 2
ATTN_BLOCK = 256
ATTN_LOOKAHEAD = 6
ATTN_SUM_ROWS = 16
MLP_ROWS = 512
MLP_COLS = 256


def _bdot(a, b):
    return jnp.dot(a.astype(BF16), b.astype(BF16), preferred_element_type=F32)


def _bdot_nt(a, b):
    return lax.dot_general(a.astype(BF16), b.astype(BF16), (((1,), (1,)), ((), ())),
                           preferred_element_type=F32)


def _bdot_tn(a, b):
    return lax.dot_general(a.astype(BF16), b.astype(BF16), (((0,), (0,)), ((), ())),
                           preferred_element_type=F32)


def _sigmoid(x):
    return 1.0 / (1.0 + jnp.exp2(x * (-math.log2(math.e))))


def _half_rms(x, w, lo):
    x2 = x * x
    s_lo = jnp.sum(jnp.where(lo, x2, 0.0), axis=-1, keepdims=True)
    s_hi = jnp.sum(jnp.where(lo, 0.0, x2), axis=-1, keepdims=True)
    ms = jnp.where(lo, s_lo, s_hi) * (1.0 / DIFF_HEAD_DIM)
    return x * lax.rsqrt(ms + EPS) * w


def _inproj_kernel(x_ref, n1_ref, wlo_ref, whi_ref, wab_ref, cw_ref, qw_ref, kw_ref, proj_ref,
                   ab_ref, ybuf, halo, *, tm, d, tiles_per_seq):
    i = pl.program_id(0)
    pad = SUBLANES
    x = x_ref[...]
    ms = jnp.mean(x * x, axis=-1, keepdims=True)
    xn = (x * lax.rsqrt(ms + EPS) * n1_ref[...]).astype(BF16)
    ab_ref[...] = jnp.dot(xn, wab_ref[...], preferred_element_type=F32)
    seq_start = (i % tiles_per_seq) == 0
    lane = lax.broadcasted_iota(jnp.int32, (INPROJ_PIECE, LANES), 1)
    lo = lane < DIFF_HEAD_DIM
    q_scale = (DIFF_HEAD_DIM ** -0.5) * math.log2(math.e)

    def matmul(item):
        seg, cq = item
        col = seg * d + cq * INPROJ_COLS
        w_ref, wcol = (wlo_ref, col) if seg < SEG_DQ else (whi_ref, col - SEG_DQ * d)
        return jnp.dot(xn, w_ref[:, wcol:wcol + INPROJ_COLS], preferred_element_type=F32)

    def epilogue(item, y):
        seg, cq = item
        lc = cq * INPROJ_COLS
        col = seg * d + lc
        if seg in (SEG_GQ, SEG_GK, SEG_GV):
            hist = jnp.where(seq_start, 0.0, halo[seg, :, lc:lc + INPROJ_COLS])
            ybuf[seg, 0:pad, lc:lc + INPROJ_COLS] = hist
            ybuf[seg, pad:pad + tm, lc:lc + INPROJ_COLS] = y
            halo[seg, :, lc:lc + INPROJ_COLS] = y[tm - pad:tm, :]
            for c0 in range(0, INPROJ_COLS, GDN_HEAD_DIM):
                for r0 in range(0, tm, INPROJ_PIECE):
                    win = ybuf[seg, r0:r0 + pad + INPROJ_PIECE, lc + c0:lc + c0 + GDN_HEAD_DIM]
                    acc = None
                    for j in range(CONV_K):
                        shift = CONV_K - 1 - j
                        tap = pltpu.roll(win, shift, axis=0) if shift else win
                        term = tap[pad:, :] * cw_ref[j:j + 1, col + c0:col + c0 + GDN_HEAD_DIM]
                        acc = term if acc is None else acc + term
                    v = acc * _sigmoid(acc)
                    if seg != SEG_GV:
                        v = v * lax.rsqrt(jnp.sum(v * v, axis=-1, keepdims=True) + EPS)
                    if seg == SEG_GQ:
                        v = v * (GDN_HEAD_DIM ** -0.5)
                    proj_ref[r0:r0 + INPROJ_PIECE, col + c0:col + c0 + GDN_HEAD_DIM] = (
                        v.astype(BF16))
        elif seg == SEG_GZ:
            proj_ref[:, col:col + INPROJ_COLS] = (y * _sigmoid(y)).astype(BF16)
        elif seg in (SEG_DQ, SEG_DK):
            w_ref_n = qw_ref if seg == SEG_DQ else kw_ref
            for c0 in range(0, INPROJ_COLS, LANES):
                for r0 in range(0, tm, INPROJ_PIECE):
                    yn = _half_rms(y[r0:r0 + INPROJ_PIECE, c0:c0 + LANES], w_ref_n[...], lo)
                    rows_p = slice(r0, r0 + INPROJ_PIECE)
                    if seg == SEG_DQ:
                        yn = yn * q_scale
                        col2 = SEG_DQ2 * d + lc + c0
                        proj_ref[rows_p, col + c0:col + c0 + LANES] = (
                            jnp.where(lo, yn, 0.0).astype(BF16))
                        proj_ref[rows_p, col2:col2 + LANES] = jnp.where(lo, 0.0, yn).astype(BF16)
                    else:
                        proj_ref[rows_p, col + c0:col + c0 + LANES] = yn.astype(BF16)
        elif seg == SEG_DV:
            proj_ref[:, col:col + INPROJ_COLS] = y.astype(BF16)
        else:
            proj_ref[:, col:col + INPROJ_COLS] = _sigmoid(y).astype(BF16)

    order = (SEG_GQ, SEG_DV, SEG_GK, SEG_GATE_A, SEG_GV, SEG_GATE_B, SEG_DQ, SEG_GZ, SEG_DK)
    items = [(seg, cq) for cq in range(d // INPROJ_COLS) for seg in order]
    ahead = [matmul(item) for item in items[:INPROJ_LOOKAHEAD]]
    for n, item in enumerate(items):
        y = ahead.pop(0)
        if n + INPROJ_LOOKAHEAD < len(items):
            ahead.append(matmul(items[n + INPROJ_LOOKAHEAD]))
        epilogue(item, y)


def _inproj(x2, n1, w_lo, w_hi, w_ab, conv_w, qw, kw, tm, seq):
    t, d = x2.shape
    fixed = lambda i: (0, 0)
    resident = pl.Buffered(1)
    kern = functools.partial(_inproj_kernel, tm=tm, d=d, tiles_per_seq=seq // tm)
    return pl.pallas_call(
        kern,
        grid=(t // tm,),
        in_specs=[
            pl.BlockSpec((tm, d), lambda i: (i, 0)),
            pl.BlockSpec((1, d), fixed),
            pl.BlockSpec(w_lo.shape, fixed, pipeline_mode=resident),
            pl.BlockSpec(w_hi.shape, fixed, pipeline_mode=resident),
            pl.BlockSpec((d, LANES), fixed),
            pl.BlockSpec((CONV_K, 3 * d), fixed),
            pl.BlockSpec((1, LANES), fixed),
            pl.BlockSpec((1, LANES), fixed),
        ],
        out_specs=[
            pl.BlockSpec((tm, NUM_OUT_SEG * d), lambda i: (i, 0)),
            pl.BlockSpec((tm, LANES), lambda i: (i, 0)),
        ],
        out_shape=[
            jax.ShapeDtypeStruct((t, NUM_OUT_SEG * d), BF16),
            jax.ShapeDtypeStruct((t, LANES), F32),
        ],
        scratch_shapes=[
            pltpu.VMEM((3, tm + SUBLANES, d), F32),
            pltpu.VMEM((3, SUBLANES, d), F32),
        ],
        compiler_params=pltpu.CompilerParams(
            dimension_semantics=("arbitrary",), vmem_limit_bytes=VMEM_LIMIT),
        name="inproj",
    )(x2, n1, w_lo, w_hi, w_ab, conv_w, qw, kw)


def _split3(x):
    h = x.astype(BF16)
    r = x - h.astype(F32)
    m = r.astype(BF16)
    l = (r - m.astype(F32)).astype(BF16)
    return h, m, l


def _gdn_kernel(q_ref, k_ref, v_ref, z_ref, ga_ref, ab_ref, hv_ref, nw_ref, o_ref, state,
                *, rows, chunk, heads):
    s = pl.program_id(1)

    @pl.when(s == 0)
    def _():
        state[...] = jnp.zeros_like(state)

    ri = lax.broadcasted_iota(jnp.int32, (chunk, chunk), 0)
    ci = lax.broadcasted_iota(jnp.int32, (chunk, chunk), 1)
    lower = ri >= ci
    strict = ri > ci
    eye = jnp.where(ri == ci, 1.0, 0.0).astype(F32)
    tri_ones = jnp.where(lower, 1.0, 0.0).astype(BF16)
    block_sizes = [2 ** e for e in range(1, int(math.log2(chunk)))]
    pair_mask = strict & ((ri >> 1) == (ci >> 1))
    couple_masks = [((ri >> (b.bit_length())) == (ci >> (b.bit_length())))
                    & ((ri & b) != 0) & ((ci & b) == 0) for b in block_sizes]
    a_neg = -jnp.exp(hv_ref[0:1, :])
    dt_bias = hv_ref[1:2, :]
    norm_w = nw_ref[...]

    cols = [h * GDN_HEAD_DIM for h in range(heads)]
    shared = {}

    def unit_stages(r0, hs):
        rows_c = slice(r0, r0 + chunk)
        if r0 not in shared:
            ab = ab_ref[rows_c, :]
            xa = ab + dt_bias
            softplus = jnp.maximum(xa, 0.0) + jnp.log1p(jnp.exp(-jnp.abs(xa)))
            g = a_neg * softplus
            beta = _sigmoid(ab)
            g1, g2, g3 = _split3(g)
            gc = (jnp.dot(tri_ones, g1, preferred_element_type=F32)
                  + jnp.dot(tri_ones, g2, preferred_element_type=F32)
                  + jnp.dot(tri_ones, g3, preferred_element_type=F32))
            shared[r0] = (gc, gc.T, beta)
        gc, gct, beta = shared[r0]
        q = {h: q_ref[rows_c, cols[h]:cols[h] + GDN_HEAD_DIM].astype(F32) for h in hs}
        k = {h: k_ref[rows_c, cols[h]:cols[h] + GDN_HEAD_DIM].astype(F32) for h in hs}
        v = {h: v_ref[rows_c, cols[h]:cols[h] + GDN_HEAD_DIM].astype(F32) for h in hs}
        gcol = {h: jnp.broadcast_to(gc[:, h:h + 1], (chunk, GDN_HEAD_DIM)) for h in hs}
        bcol = {h: jnp.broadcast_to(beta[:, heads + h:heads + h + 1], (chunk, GDN_HEAD_DIM))
                for h in hs}
        glast = {h: jnp.broadcast_to(gc[chunk - 1:chunk, h:h + 1], (chunk, GDN_HEAD_DIM))
                 for h in hs}
        decay = {h: jnp.exp(jnp.where(lower, gcol[h] - jnp.broadcast_to(gct[h:h + 1, :],
                                                                     (chunk, chunk)), 0.0))
                 for h in hs}
        gram = {h: _bdot_nt(jnp.concatenate([k[h], q[h]], axis=0), k[h]) for h in hs}
        yield
        a_qk = {h: jnp.where(lower, gram[h][chunk:] * decay[h], 0.0) for h in hs}
        neg_a = {h: jnp.where(strict, -(bcol[h] * gram[h][:chunk] * decay[h]), 0.0) for h in hs}
        t_inv = {h: eye + jnp.where(pair_mask, neg_a[h], 0.0) for h in hs}
        for couple_mask in couple_masks:
            x = {h: _bdot(jnp.where(couple_mask, neg_a[h], 0.0), t_inv[h]) for h in hs}
            yield
            t_inv = {h: t_inv[h] + _bdot(t_inv[h], x[h]) for h in hs}
            yield
        eg = {h: jnp.exp(gcol[h]) for h in hs}
        uw = {h: _bdot(t_inv[h], jnp.concatenate([v[h] * bcol[h], k[h] * bcol[h] * eg[h]], axis=1))
              for h in hs}
        yield
        st = {h: state[h] for h in hs}
        ws = {h: _bdot(jnp.concatenate([uw[h][:, GDN_HEAD_DIM:], q[h] * eg[h]], axis=0), st[h])
              for h in hs}
        yield
        v_new = {h: uw[h][:, :GDN_HEAD_DIM] - ws[h][:chunk] for h in hs}
        o = {h: ws[h][chunk:] + _bdot(a_qk[h], v_new[h]) for h in hs}
        for h in hs:
            k_dec = k[h] * jnp.exp(glast[h] - gcol[h])
            state[h] = st[h] * jnp.exp(glast[h]) + _bdot_tn(k_dec, v_new[h])
        yield
        for h in hs:
            c0 = cols[h]
            on = o[h] * lax.rsqrt(jnp.mean(o[h] * o[h], axis=-1, keepdims=True) + EPS) * norm_w
            z = z_ref[rows_c, c0:c0 + GDN_HEAD_DIM].astype(F32)
            gate = ga_ref[rows_c, c0:c0 + GDN_HEAD_DIM].astype(F32)
            o_ref[rows_c, c0:c0 + GDN_HEAD_DIM] = (on * z * gate).astype(BF16)
        yield

    n_stages = 2 * len(block_sizes) + 5
    groups = [list(range(g0, g0 + GDN_GROUP)) for g0 in range(0, heads, GDN_GROUP)]
    assert GDN_SKEW * len(groups) >= 2
    gens = [unit_stages(c * chunk, hs) for c in range(rows // chunk) for hs in groups]
    for step in range(GDN_SKEW * (len(gens) - 1) + n_stages):
        for u, gen in enumerate(gens):
            if 0 <= step - GDN_SKEW * u < n_stages:
                next(gen)


def _gdn(proj, ab, head_vecs, norm_w, batch, seq, heads):
    t = proj.shape[0]
    width = heads * GDN_HEAD_DIM
    rows = GDN_ROWS
    nsb = seq // rows

    def seg_spec(seg):
        return pl.BlockSpec((rows, width), lambda b, s, seg=seg: (b * nsb + s, seg))

    kern = functools.partial(_gdn_kernel, rows=rows, chunk=GDN_CHUNK, heads=heads)
    return pl.pallas_call(
        kern,
        grid=(batch, nsb),
        in_specs=[
            seg_spec(SEG_GQ), seg_spec(SEG_GK), seg_spec(SEG_GV), seg_spec(SEG_GZ),
            seg_spec(SEG_GATE_A),
            pl.BlockSpec((rows, LANES), lambda b, s: (b * nsb + s, 0)),
            pl.BlockSpec((SUBLANES, LANES), lambda b, s: (0, 0)),
            pl.BlockSpec((1, GDN_HEAD_DIM), lambda b, s: (0, 0)),
        ],
        out_specs=pl.BlockSpec((rows, width), lambda b, s: (b * nsb + s, 0)),
        out_shape=jax.ShapeDtypeStruct((t, width), BF16),
        scratch_shapes=[pltpu.VMEM((heads, GDN_HEAD_DIM, GDN_HEAD_DIM), F32)],
        compiler_params=pltpu.CompilerParams(
            dimension_semantics=("parallel", "arbitrary"), vmem_limit_bytes=VMEM_LIMIT),
        name="gdn",
    )(proj, proj, proj, proj, proj, ab, head_vecs, norm_w)


def _attn_kernel(q1_ref, q2_ref, k_ref, v_ref, gate_ref, lam_ref, sw_ref, o_ref, vt_ref,
                 *, blk, seq):
    hd = 2 * DIFF_HEAD_DIM
    nblk = seq // blk
    q_refs = (q1_ref, q2_ref)

    for c in range(seq // hd):
        rows = slice(c * hd, (c + 1) * hd)
        vt_ref[0:hd, rows] = v_ref[rows, :].astype(F32).T.astype(BF16)
    ones_row = lax.broadcasted_iota(jnp.int32, (ATTN_SUM_ROWS, seq), 0) == 0
    vt_ref[hd:hd + ATTN_SUM_ROWS, :] = jnp.where(ones_row, 1.0, 0.0).astype(BF16)

    lp = lam_ref[...]
    lam = (jnp.exp(jnp.sum(lp[0:1, :] * lp[1:2, :], axis=-1, keepdims=True))
           - jnp.exp(jnp.sum(lp[2:3, :] * lp[3:4, :], axis=-1, keepdims=True)) + LAMBDA_INIT)
    kv_idx = lax.broadcasted_iota(jnp.int32, (blk, blk), 0)
    q_idx = lax.broadcasted_iota(jnp.int32, (blk, blk), 1)
    causal = kv_idx <= q_idx

    def scores(item):
        i, j, c = item
        return lax.dot_general(k_ref[j * blk:(j + 1) * blk, :],
                               q_refs[c][i * blk:(i + 1) * blk, :], (((1,), (1,)), ((), ())),
                               preferred_element_type=F32)

    items = [(i, j, c) for i in range(nblk) for j in range(i + 1) for c in range(2)]
    ahead = [scores(items[n]) for n in range(ATTN_LOOKAHEAD)]
    stats = [None, None]
    for n, (i, j, c) in enumerate(items):
        sc = ahead.pop(0)
        if n + ATTN_LOOKAHEAD < len(items):
            ahead.append(scores(items[n + ATTN_LOOKAHEAD]))
        if j == i:
            sc = jnp.where(causal, sc, -jnp.inf)
        vt = vt_ref[:, j * blk:(j + 1) * blk]
        if j == 0:
            m = jnp.max(sc, axis=0, keepdims=True)
            p = jnp.exp2(sc - m)
            acc = jnp.dot(vt, p.astype(BF16), preferred_element_type=F32)
        else:
            m, acc = stats[c]
            m_new = jnp.maximum(m, jnp.max(sc, axis=0, keepdims=True))
            alpha = jnp.exp2(m - m_new)
            p = jnp.exp2(sc - m_new)
            acc = alpha * acc + jnp.dot(vt, p.astype(BF16), preferred_element_type=F32)
            m = m_new
        stats[c] = (m, acc)
        if j == i and c == 1:
            rows = slice(i * blk, (i + 1) * blk)
            on1 = stats[0][1][:hd] / stats[0][1][hd:hd + 1]
            on2 = stats[1][1][:hd] / stats[1][1][hd:hd + 1]
            o = (on1 - lam * on2).T
            o = o * lax.rsqrt(jnp.mean(o * o, axis=-1, keepdims=True) + EPS) * sw_ref[...]
            o = o * (1.0 - LAMBDA_INIT)
            o_ref[rows, :] = (o * gate_ref[rows, :].astype(F32)).astype(BF16)


def _attn(proj, lam, sw, batch, seq, heads):
    t = proj.shape[0]
    blk = ATTN_BLOCK
    hd = 2 * DIFF_HEAD_DIM
    kern = functools.partial(_attn_kernel, blk=blk, seq=seq)
    fixed = lambda b, h: (0, 0)
    return pl.pallas_call(
        kern,
        grid=(batch, heads),
        in_specs=[
            pl.BlockSpec((seq, hd), lambda b, h: (b, SEG_DQ * heads + h)),
            pl.BlockSpec((seq, hd), lambda b, h: (b, SEG_DQ2 * heads + h)),
            pl.BlockSpec((seq, hd), lambda b, h: (b, SEG_DK * heads + h)),
            pl.BlockSpec((seq, hd), lambda b, h: (b, SEG_DV * heads + h)),
            pl.BlockSpec((seq, hd), lambda b, h: (b, SEG_GATE_B * heads + h)),
            pl.BlockSpec((SUBLANES, hd), fixed),
            pl.BlockSpec((1, hd), fixed),
        ],
        out_specs=pl.BlockSpec((seq, hd), lambda b, h: (b, h)),
        out_shape=jax.ShapeDtypeStruct((t, heads * hd), BF16),
        scratch_shapes=[pltpu.VMEM((hd + ATTN_SUM_ROWS, seq), BF16)],
        compiler_params=pltpu.CompilerParams(
            dimension_semantics=("parallel", "parallel"), vmem_limit_bytes=VMEM_LIMIT),
        name="diffattn",
    )(proj, proj, proj, proj, proj, lam, sw)


def _mlp_kernel(oa_ref, ob_ref, x_ref, wo_ref, n2_ref, wg_ref, wu_ref, wd_ref, o_ref, u_ref,
                *, tf):
    dff = wd_ref.shape[0]
    mixed =(oa_ref[...].astype(F32) + ob_ref[...].astype(F32)).astype(BF16)
    d = x_ref.shape[1]
    h = x_ref[...] + jnp.dot(mixed, wo_ref[:, :d], preferred_element_type=F32)
    ms = jnp.mean(h * h, axis=-1, keepdims=True)
    u_ref[...] = (h * lax.rsqrt(ms + EPS) * n2_ref[...]).astype(BF16)

    def gate_up(f):
        u = u_ref[...]
        return (jnp.dot(u, wg_ref[:, f:f + tf], preferred_element_type=F32),
                jnp.dot(u, wu_ref[:, f:f + tf], preferred_element_type=F32))

    acc = h
    starts = list(range(0, dff, tf))
    nxt = gate_up(starts[0])
    for n, f in enumerate(starts):
        g, up = nxt
        if n + 1 < len(starts):
            nxt = gate_up(starts[n + 1])
        act = (g * _sigmoid(g) * up).astype(BF16)
        acc = acc + jnp.dot(act, wd_ref[f:f + tf, :d], preferred_element_type=F32)
    o_ref[...] = acc


def _mlp(oa, ob, x2, w_out, n2, wg, wu, wd, tm, tf):
    t, d = x2.shape
    row = lambda i: (i, 0)
    fixed = lambda i: (0, 0)
    resident = pl.Buffered(1)
    return pl.pallas_call(
        functools.partial(_mlp_kernel, tf=tf),
        grid=(t // tm,),
        in_specs=[
            pl.BlockSpec((tm, d), row), pl.BlockSpec((tm, d), row), pl.BlockSpec((tm, d), row),
            pl.BlockSpec(w_out.shape, fixed, pipeline_mode=resident),
            pl.BlockSpec((1, d), fixed),
            pl.BlockSpec(wg.shape, fixed, pipeline_mode=resident),
            pl.BlockSpec(wu.shape, fixed, pipeline_mode=resident),
            pl.BlockSpec(wd.shape, fixed, pipeline_mode=resident),
        ],
        out_specs=pl.BlockSpec((tm, d), row),
        out_shape=jax.ShapeDtypeStruct((t, d), F32),
        scratch_shapes=[pltpu.VMEM((tm, d), BF16)],
        compiler_params=pltpu.CompilerParams(
            dimension_semantics=("parallel",), vmem_limit_bytes=VMEM_LIMIT),
        name="mlp",
    )(oa, ob, x2, w_out, n2, wg, wu, wd)


def _resident_weight(w):
    tiles = -(-w.shape[1] // MXU_COLS)
    width = (tiles + 1 - tiles % 2) * MXU_COLS
    return jnp.pad(w.astype(BF16), ((0, 0), (0, width - w.shape[1])))


def _layer(h2, batch, seq, norm1_w, w_in, conv_w, a_log, dt_bias, gdn_norm_w, q_norm_w, k_norm_w,
           lambda_q1, lambda_k1, lambda_q2, lambda_k2, subln_w, w_out, norm2_w, w_gate, w_up,
           w_down):
    d = h2.shape[1]
    heads = d // GDN_HEAD_DIM
    c0 = 4 * d
    c1 = c0 + 2 * heads
    w_lo = _resident_weight(w_in[:, :c0])
    w_hi = _resident_weight(w_in[:, c1:])
    w_ab = jnp.pad(w_in[:, c0:c1], ((0, 0), (0, LANES - 2 * heads))).astype(BF16)
    head_vecs = jnp.zeros((SUBLANES, LANES), F32)
    head_vecs = head_vecs.at[0, :heads].set(a_log).at[1, :heads].set(dt_bias)
    lam = jnp.zeros((SUBLANES, 2 * DIFF_HEAD_DIM), F32)
    lam = (lam.at[0, :DIFF_HEAD_DIM].set(lambda_q1).at[1, :DIFF_HEAD_DIM].set(lambda_k1)
           .at[2, :DIFF_HEAD_DIM].set(lambda_q2).at[3, :DIFF_HEAD_DIM].set(lambda_k2))
    qw = jnp.tile(q_norm_w, 2)[None, :]
    kw = jnp.tile(k_norm_w, 2)[None, :]

    proj, ab = _inproj(h2, norm1_w[None, :], w_lo, w_hi, w_ab, conv_w, qw, kw, tm=INPROJ_ROWS,
                       seq=seq)
    oa = _gdn(proj, ab, head_vecs, gdn_norm_w[None, :], batch, seq, heads)
    ob = _attn(proj, lam, subln_w[None, :], batch, seq, heads)
    return _mlp(oa, ob, h2, _resident_weight(w_out), norm2_w[None, :], _resident_weight(w_gate),
                _resident_weight(w_up), _resident_weight(w_down), tm=MLP_ROWS, tf=MLP_COLS)


def kernel(x, norm1_w, w_in, conv_w, a_log, dt_bias, gdn_norm_w, q_norm_w, k_norm_w, lambda_q1,
           lambda_k1, lambda_q2, lambda_k2, subln_w, w_out, norm2_w, w_gate, w_up, w_down):
    batch, seq, d = x.shape
    depth = norm1_w.shape[0]
    assert depth == 1, "lambda_init is specialised to a single layer"
    h2 = x.reshape(batch * seq, d)
    for l in range(depth):
        h2 = _layer(h2, batch, seq, norm1_w[l], w_in[l], conv_w[l], a_log[l], dt_bias[l],
                    gdn_norm_w[l], q_norm_w[l], k_norm_w[l], lambda_q1[l], lambda_k1[l],
                    lambda_q2[l], lambda_k2[l], subln_w[l], w_out[l], norm2_w[l], w_gate[l],
                    w_up[l], w_down[l])
    return h2.reshape(batch, seq, d)
```
